```python
import math
import jax, jax.numpy as jnp
from jax import lax
import numpy as np

D_MODEL = 1024
BATCH = 32
SEQ = 2048
DEPTH = 1

CHUNK = 64
D_S5 = D_MODEL // 2
D_RWKV = D_MODEL - D_S5
S5_GROUP = 16
S5_GROUPS = D_S5 // S5_GROUP
S5_STATE = 64
RWKV_HEAD = 64
RWKV_HEADS = D_RWKV // RWKV_HEAD
W_RANK = 64
A_RANK = 64
G_RANK = 128
N_RW_COLS = 3 * D_RWKV + W_RANK + A_RANK + G_RANK
N_PROJ = D_S5 + N_RW_COLS
D_FF = 4 * D_MODEL
RMS_EPS = 1e-6
GN_EPS = 64e-5
DT_MIN = 1e-3
DT_MAX = 1e-1

kernel_name = "hymba_s5_rwkv7_sandwich_block"


def rmsnorm(x, g):
    xf = x.astype(jnp.float32)
    y = xf * lax.rsqrt(jnp.mean(xf * xf, axis=-1, keepdims=True) + RMS_EPS)
    return (y * g.astype(jnp.float32)).astype(x.dtype)


def _s5_combine(e1, e2):
    a1r, a1i, b1r, b1i = e1
    a2r, a2i, b2r, b2i = e2
    return (a1r * a2r - a1i * a2i,
            a1r * a2i + a1i * a2r,
            a2r * b1r - a2i * b1i + b2r,
            a2r * b1i + a2i * b1r + b2i)


def s5_mixer(u, lam_re, lam_im, log_dt, b_re, b_im, c_re, c_im, d_skip, w_glu, b_glu):
    f32 = jnp.float32
    Bsz, L, _ = u.shape
    uf = u.astype(f32).reshape(Bsz, L, S5_GROUPS, S5_GROUP)
    dt = jnp.exp(log_dt.astype(f32))[:, None]
    lr, li = lam_re.astype(f32), lam_im.astype(f32)
    mag = jnp.exp(lr * dt)
    ab_re = mag * jnp.cos(li * dt)
    ab_im = mag * jnp.sin(li * dt)
    den = lr * lr + li * li
    nr = ab_re - 1.0
    f_re = (nr * lr + ab_im * li) / den
    f_im = (ab_im * lr - nr * li) / den
    br, bi = b_re.astype(f32), b_im.astype(f32)
    bb_re = f_re[..., None] * br - f_im[..., None] * bi
    bb_im = f_re[..., None] * bi + f_im[..., None] * br
    bu_re = jnp.einsum('blgc,gpc->blgp', uf, bb_re)
    bu_im = jnp.einsum('blgc,gpc->blgp', uf, bb_im)
    a_re = jnp.broadcast_to(ab_re[None, None], (1, L, S5_GROUPS, S5_STATE))
    a_im = jnp.broadcast_to(ab_im[None, None], (1, L, S5_GROUPS, S5_STATE))
    _, _, s_re, s_im = lax.associative_scan(_s5_combine, (a_re, a_im, bu_re, bu_im), axis=1)
    y = (jnp.einsum('blgp,gcp->blgc', s_re, c_re.astype(f32))
         - jnp.einsum('blgp,gcp->blgc', s_im, c_im.astype(f32)))
    y = y.reshape(Bsz, L, D_S5) + d_skip.astype(f32) * uf.reshape(Bsz, L, D_S5)
    z = jax.nn.gelu(y)
    out = z * jax.nn.sigmoid(z @ w_glu.astype(f32) + b_glu.astype(f32))
    return out.astype(u.dtype)


def _rwkv7_recurrence(r, w, k, v, kk, a):
    Bsz, L, H, N = r.shape
    n_chunks = L // CHUNK

    def to_chunks(t):
        return jnp.moveaxis(t, 1, 0).reshape(n_chunks, CHUNK, Bsz, H, N)

    xs = (to_chunks(r), to_chunks(w), to_chunks(k), to_chunks(v), to_chunks(kk), to_chunks(a))

    def step(S, inp):
        r_t, w_t, k_t, v_t, kk_t, a_t = inp
        sa = jnp.einsum('bhvk,bhk->bhv', S, -kk_t)
        S = (S * w_t[:, :, None, :]
             + sa[..., None] * (kk_t * a_t)[:, :, None, :]
             + v_t[..., None] * k_t[:, :, None, :])
        return S, jnp.einsum('bhvk,bhk->bhv', S, r_t)

    def chunk_step(S, chunk):
        return lax.scan(step, S, chunk)

    S0 = jnp.zeros((Bsz, H, N, N), r.dtype)
    _, o = lax.scan(chunk_step, S0, xs)
    return jnp.moveaxis(o.reshape(L, Bsz, H, N), 0, 1)


def rwkv7_mixer(q, mu, w0, w_up, a0, a_up, g_up, k_k, k_a, r_k, gn_w, gn_b):
    f32 = jnp.float32
    Bsz, L, _ = q.shape
    H, N = RWKV_HEADS, RWKV_HEAD
    qf = q.astype(f32)
    prev = jnp.pad(qf, ((0, 0), (1, 0), (0, 0)))[:, :-1]
    qs = qf + mu.astype(f32) * (prev - qf)
    o1, o2, o3 = D_RWKV, 2 * D_RWKV, 3 * D_RWKV
    o4, o5 = o3 + W_RANK, o3 + W_RANK + A_RANK
    r, k, v = qs[..., :o1], qs[..., o1:o2], qs[..., o2:o3]
    xw, xa, xg = qs[..., o3:o4], qs[..., o4:o5], qs[..., o5:]
    w_log = -jax.nn.softplus(-(w0.astype(f32) + jnp.tanh(xw) @ w_up.astype(f32))) - 0.5
    decay = jnp.exp(-jnp.exp(w_log))
    a = jax.nn.sigmoid(a0.astype(f32) + xa @ a_up.astype(f32))
    g = jax.nn.sigmoid(xg) @ g_up.astype(f32)
    hd = lambda t: t.reshape(Bsz, L, H, N)
    kk = hd(k * k_k.astype(f32))
    kk = kk / jnp.maximum(jnp.linalg.norm(kk, axis=-1, keepdims=True), 1e-12)
    k = k * (1.0 + (a - 1.0) * k_a.astype(f32))
    rh, kh, vh, ah = hd(r), hd(k), hd(v), hd(a)
    o = _rwkv7_recurrence(rh, hd(decay), kh, vh, kk, ah)
    mean = jnp.mean(o, axis=-1, keepdims=True)
    var = jnp.mean(jnp.square(o - mean), axis=-1, keepdims=True)
    o = ((o - mean) * lax.rsqrt(var + GN_EPS)).reshape(Bsz, L, D_RWKV)
    o = o * gn_w.astype(f32) + gn_b.astype(f32)
    bonus = jnp.sum(rh * kh * r_k.astype(f32), axis=-1, keepdims=True) * vh
    out = (o + bonus.reshape(Bsz, L, D_RWKV)) * g
    return out.astype(q.dtype)


def hybrid_layer(x, g_pre_mix, w_in, s5_lam_re, s5_lam_im, s5_log_dt, s5_b_re, s5_b_im,
                 s5_c_re, s5_c_im, s5_d, s5_w_glu, s5_b_glu, rw_mu, rw_w0, rw_w_up, rw_a0,
                 rw_a_up, rw_g_up, rw_k_k, rw_k_a, rw_r_k, rw_gn_w, rw_gn_b, w_out,
                 g_post_mix, g_pre_mlp, w_ff_up, w_ff_down, g_post_mlp):
    xn = rmsnorm(x, g_pre_mix)
    p = xn @ w_in
    y_s5 = s5_mixer(p[..., :D_S5], s5_lam_re, s5_lam_im, s5_log_dt, s5_b_re, s5_b_im,
                    s5_c_re, s5_c_im, s5_d, s5_w_glu, s5_b_glu)
    y_rw = rwkv7_mixer(p[..., D_S5:], rw_mu, rw_w0, rw_w_up, rw_a0, rw_a_up, rw_g_up,
                       rw_k_k, rw_k_a, rw_r_k, rw_gn_w, rw_gn_b)
    mix = jnp.concatenate([y_s5, y_rw], axis=-1) @ w_out
    h = x + rmsnorm(mix, g_post_mix)
    hn = rmsnorm(h, g_pre_mlp)
    ff = jnp.square(jax.nn.relu(hn @ w_ff_up)) @ w_ff_down
    return h + rmsnorm(ff, g_post_mlp)


def setup_inputs(seed: int = 0) -> dict:
    key = jax.random.key(seed)
    ks = jax.random.split(key, 32)
    f32 = jnp.float32
    L = DEPTH

    def nrm(k, shape, scale):
        return jax.random.normal(k, shape, f32) * scale

    def gain(k, shape):
        return 1.0 + 0.02 * jax.random.normal(k, shape, f32)

    x = jax.random.normal(ks[0], (BATCH, SEQ, D_MODEL), f32)
    lam_im = (jnp.pi * jnp.arange(S5_STATE, dtype=f32))[None, None, :] \
        + 0.01 * jax.random.normal(ks[4], (L, S5_GROUPS, S5_STATE), f32)
    log_dt = jax.random.uniform(ks[5], (L, S5_GROUPS), f32,
                                math.log(DT_MIN), math.log(DT_MAX))
    return {
        "x": x,
        "g_pre_mix": gain(ks[1], (L, D_MODEL)),
        "w_in": nrm(ks[2], (L, D_MODEL, N_PROJ), D_MODEL ** -0.5),
        "s5_lam_re": -0.5 + 0.01 * jax.random.normal(ks[3], (L, S5_GROUPS, S5_STATE), f32),
        "s5_lam_im": lam_im,
        "s5_log_dt": log_dt,
        "s5_b_re": nrm(ks[6], (L, S5_GROUPS, S5_STATE, S5_GROUP), (2 * S5_GROUP) ** -0.5),
        "s5_b_im": nrm(ks[7], (L, S5_GROUPS, S5_STATE, S5_GROUP), (2 * S5_GROUP) ** -0.5),
        "s5_c_re": nrm(ks[8], (L, S5_GROUPS, S5_GROUP, S5_STATE), (2 * S5_STATE) ** -0.5),
        "s5_c_im": nrm(ks[9], (L, S5_GROUPS, S5_GROUP, S5_STATE), (2 * S5_STATE) ** -0.5),
        "s5_d": nrm(ks[10], (L, D_S5), 1.0),
        "s5_w_glu": nrm(ks[11], (L, D_S5, D_S5), D_S5 ** -0.5),
        "s5_b_glu": nrm(ks[12], (L, D_S5), 0.01),
        "rw_mu": jax.random.uniform(ks[13], (L, N_RW_COLS), f32),
        "rw_w0": nrm(ks[14], (L, D_RWKV), 0.5),
        "rw_w_up": nrm(ks[15], (L, W_RANK, D_RWKV), W_RANK ** -0.5),
        "rw_a0": nrm(ks[16], (L, D_RWKV), 0.1),
        "rw_a_up": nrm(ks[17], (L, A_RANK, D_RWKV), A_RANK ** -0.5),
        "rw_g_up": nrm(ks[18], (L, G_RANK, D_RWKV), G_RANK ** -0.5),
        "rw_k_k": 0.85 + 0.02 * jax.random.normal(ks[19], (L, D_RWKV), f32),
        "rw_k_a": gain(ks[20], (L, D_RWKV)),
        "rw_r_k": nrm(ks[21], (L, RWKV_HEADS, RWKV_HEAD), 0.1),
        "rw_gn_w": gain(ks[22], (L, D_RWKV)),
        "rw_gn_b": nrm(ks[23], (L, D_RWKV), 0.01),
        "w_out": nrm(ks[24], (L, D_MODEL, D_MODEL), D_MODEL ** -0.5),
        "g_post_mix": gain(ks[25], (L, D_MODEL)),
        "g_pre_mlp": gain(ks[26], (L, D_MODEL)),
        "w_ff_up": nrm(ks[27], (L, D_MODEL, D_FF), D_MODEL ** -0.5),
        "w_ff_down": nrm(ks[28], (L, D_FF, D_MODEL), D_FF ** -0.5),
        "g_post_mlp": gain(ks[29], (L, D_MODEL)),
    }


def reference(x, g_pre_mix, w_in, s5_lam_re, s5_lam_im, s5_log_dt, s5_b_re, s5_b_im,
              s5_c_re, s5_c_im, s5_d, s5_w_glu, s5_b_glu, rw_mu, rw_w0, rw_w_up, rw_a0,
              rw_a_up, rw_g_up, rw_k_k, rw_k_a, rw_r_k, rw_gn_w, rw_gn_b, w_out,
              g_post_mix, g_pre_mlp, w_ff_up, w_ff_down, g_post_mlp):
    h = x
    for l in range(DEPTH):
        h = hybrid_layer(h, g_pre_mix[l], w_in[l], s5_lam_re[l], s5_lam_im[l], s5_log_dt[l],
                         s5_b_re[l], s5_b_im[l], s5_c_re[l], s5_c_im[l], s5_d[l],
                         s5_w_glu[l], s5_b_glu[l], rw_mu[l], rw_w0[l], rw_w_up[l], rw_a0[l],
                         rw_a_up[l], rw_g_up[l], rw_k_k[l], rw_k_a[l], rw_r_k[l],
                         rw_gn_w[l], rw_gn_b[l], w_out[l], g_post_mix[l], g_pre_mlp[l],
                         w_ff_up[l], w_ff_down[l], g_post_mlp[l])
    return h
```

```python
import functools
import math

import jax
import jax.numpy as jnp
from jax import lax
from jax.experimental import pallas as pl
from jax.experimental.pallas import tpu as pltpu

F32 = jnp.float32
BF16 = jnp.bfloat16

D_MODEL = 1024
D_S5 = 512
D_RWKV = 512
S5_GROUP = 16
S5_GROUPS = 32
S5_STATE = 64
RWKV_HEAD = 64
RWKV_HEADS = 8
W_RANK = 64
A_RANK = 64
G_RANK = 128
N_RW_COLS = 3 * D_RWKV + W_RANK + A_RANK + G_RANK
N_PROJ = D_S5 + N_RW_COLS
D_FF = 4 * D_MODEL
RMS_EPS = 1e-6
GN_EPS = 64e-5

SUBLANES = 8
LANES = 128

ROW_TILE = 512
S5_BATCH = SUBLANES
S5_T = 64
S5_HALF_GROUPS = S5_GROUPS // 2
S5_HALF_IN = S5_HALF_GROUPS * S5_GROUP
S5_HALF_STATE = S5_HALF_GROUPS * S5_STATE
S5_SCAN_LANES = 512
RW_CHUNK = 64
RW_PAIRS = RWKV_HEADS // 2
VMEM_LIMIT = 56 * 1024 * 1024


def _rms(x, g):
    return x * lax.rsqrt(jnp.mean(x * x, axis=-1, keepdims=True) + RMS_EPS) * g


def _dot(a, b):
    return jnp.dot(a.astype(BF16), b.astype(BF16), preferred_element_type=F32)


def _inproj_kernel(x_ref, g_ref, w_ref, u_ref, q_ref):
    xn = _rms(x_ref[0], g_ref[...])
    p = _dot(xn, w_ref[...])
    u_ref[...] = p[:, :D_S5]
    q_ref[0] = p[:, D_S5:]


def _inproj(x, g, w_in):
    B, L, _ = x.shape
    tl = min(ROW_TILE, L)
    u, q = pl.pallas_call(
        _inproj_kernel,
        grid=(B, L // tl),
        in_specs=[
            pl.BlockSpec((1, tl, D_MODEL), lambda b, i: (b, i, 0)),
            pl.BlockSpec((1, D_MODEL), lambda b, i: (0, 0)),
            pl.BlockSpec((D_MODEL, N_PROJ), lambda b, i: (0, 0)),
        ],
        out_specs=[
            pl.BlockSpec((tl, D_S5), lambda b, i: (i, b)),
            pl.BlockSpec((1, tl, N_RW_COLS), lambda b, i: (b, i, 0)),
        ],
        out_shape=[
            jax.ShapeDtypeStruct((L, B * D_S5), F32),
            jax.ShapeDtypeStruct((B, L, N_RW_COLS), F32),
        ],
        compiler_params=pltpu.CompilerParams(
            dimension_semantics=("parallel", "parallel"), vmem_limit_bytes=VMEM_LIMIT),
        name="inproj",
    )(x, g, w_in)
    return u.reshape(L, B, D_S5), q


def _s5_kernel(u_ref, are_ref, aim_ref, b2_ref, c2_ref, d_ref, wglu_ref, bglu_ref,
               y_ref, s_scr, bu_scr, *, T):
    rows = T * S5_BATCH
    half_cols = 2 * S5_HALF_STATE

    @pl.when(pl.program_id(1) == 0)
    def _():
        s_scr[...] = jnp.zeros_like(s_scr)

    u = u_ref[...].reshape(rows, D_S5)
    for h in range(2):
        bu_scr[:, h * half_cols:(h + 1) * half_cols] = _dot(
            u[:, h * S5_HALF_IN:(h + 1) * S5_HALF_IN], b2_ref[h])

    for h in range(2):
        for j in range(S5_HALF_STATE // S5_SCAN_LANES):
            re0 = h * half_cols + j * S5_SCAN_LANES
            im0 = re0 + S5_HALF_STATE
            a0 = h * S5_HALF_STATE + j * S5_SCAN_LANES
            are = jnp.broadcast_to(are_ref[:, a0:a0 + S5_SCAN_LANES], (S5_BATCH, S5_SCAN_LANES))
            aim = jnp.broadcast_to(aim_ref[:, a0:a0 + S5_SCAN_LANES], (S5_BATCH, S5_SCAN_LANES))

            def body(t, carry, re0=re0, im0=im0, are=are, aim=aim):
                sre, sim = carry
                r0 = pl.multiple_of(t * S5_BATCH, S5_BATCH)
                bre = bu_scr[pl.ds(r0, S5_BATCH), re0:re0 + S5_SCAN_LANES]
                bim = bu_scr[pl.ds(r0, S5_BATCH), im0:im0 + S5_SCAN_LANES]
                nre = are * sre - aim * sim + bre
                nim = are * sim + aim * sre + bim
                bu_scr[pl.ds(r0, S5_BATCH), re0:re0 + S5_SCAN_LANES] = nre
                bu_scr[pl.ds(r0, S5_BATCH), im0:im0 + S5_SCAN_LANES] = nim
                return nre, nim

            sre, sim = lax.fori_loop(
                0, T, body,
                (s_scr[:, re0:re0 + S5_SCAN_LANES], s_scr[:, im0:im0 + S5_SCAN_LANES]),
                unroll=8)
            s_scr[:, re0:re0 + S5_SCAN_LANES] = sre
            s_scr[:, im0:im0 + S5_SCAN_LANES] = sim

    y = jnp.concatenate(
        [_dot(bu_scr[:, h * half_cols:(h + 1) * half_cols], c2_ref[h]) for h in range(2)],
        axis=-1)
    y = y + d_ref[...] * u
    z = jax.nn.gelu(y)
    gate = jax.nn.sigmoid(_dot(z, wglu_ref[...]) + bglu_ref[...])
    y_ref[...] = (z * gate).reshape(T, S5_BATCH, D_S5)


def _s5_params(lam_re, lam_im, log_dt, b_re, b_im, c_re, c_im):
    dt = jnp.exp(log_dt)[:, None]
    mag = jnp.exp(lam_re * dt)
    ab_re = mag * jnp.cos(lam_im * dt)
    ab_im = mag * jnp.sin(lam_im * dt)
    den = lam_re * lam_re + lam_im * lam_im
    nr = ab_re - 1.0
    f_re = (nr * lam_re + ab_im * lam_im) / den
    f_im = (ab_im * lam_re - nr * lam_im) / den
    bb_re = f_re[..., None] * b_re - f_im[..., None] * b_im
    bb_im = f_re[..., None] * b_im + f_im[..., None] * b_re
    eye = jnp.eye(S5_HALF_GROUPS, dtype=F32)

    def half(t, h):
        return t[h * S5_HALF_GROUPS:(h + 1) * S5_HALF_GROUPS]

    def b_block(bb, h):
        return jnp.einsum("gpc,gk->gckp", half(bb, h), eye).reshape(S5_HALF_IN, S5_HALF_STATE)

    def c_block(cc, h):
        return jnp.einsum("gcp,gk->gpkc", half(cc, h), eye).reshape(S5_HALF_STATE, S5_HALF_IN)

    b2 = jnp.stack([jnp.concatenate([b_block(bb_re, h), b_block(bb_im, h)], axis=1)
                    for h in range(2)]).astype(BF16)
    c2 = jnp.stack([jnp.concatenate([c_block(c_re, h), -c_block(c_im, h)], axis=0)
                    for h in range(2)]).astype(BF16)
    return ab_re.reshape(1, -1), ab_im.reshape(1, -1), b2, c2


def _s5(u_tm, a_re, a_im, b2, c2, d_skip, w_glu, b_glu):
    L, B, _ = u_tm.shape
    T = min(S5_T, L)
    n_state = 2 * 2 * S5_HALF_STATE
    const = lambda *shape: pl.BlockSpec(shape, lambda b, t: (0,) * len(shape))
    return pl.pallas_call(
        functools.partial(_s5_kernel, T=T),
        grid=(B // S5_BATCH, L // T),
        in_specs=[
            pl.BlockSpec((T, S5_BATCH, D_S5), lambda b, t: (t, b, 0)),
            const(1, S5_GROUPS * S5_STATE),
            const(1, S5_GROUPS * S5_STATE),
            const(2, S5_HALF_IN, 2 * S5_HALF_STATE),
            const(2, 2 * S5_HALF_STATE, S5_HALF_IN),
            const(1, D_S5),
            const(D_S5, D_S5),
            const(1, D_S5),
        ],
        out_specs=pl.BlockSpec((T, S5_BATCH, D_S5), lambda b, t: (t, b, 0)),
        out_shape=jax.ShapeDtypeStruct((L, B, D_S5), F32),
        scratch_shapes=[
            pltpu.VMEM((S5_BATCH, n_state), F32),
            pltpu.VMEM((T * S5_BATCH, n_state), F32),
        ],
        compiler_params=pltpu.CompilerParams(
            dimension_semantics=("parallel", "arbitrary"), vmem_limit_bytes=VMEM_LIMIT),
        name="s5",
    )(u_tm, a_re, a_im, b2, c2, d_skip, w_glu, b_glu)


def _rwkv_kernel(q_ref, mu_ref, w0_ref, wwa_ref, a0_ref, gup_ref, kk_ref, ka_ref, rk_ref,
                 gnw_ref, gnb_ref, o_ref, s_scr, prev_scr, *, C, Bt):
    @pl.when(pl.program_id(1) == 0)
    def _():
        s_scr[...] = jnp.zeros_like(s_scr)
        prev_scr[...] = jnp.zeros_like(prev_scr)

    row = lax.broadcasted_iota(jnp.int32, (C, LANES), 0)
    lane = lax.broadcasted_iota(jnp.int32, (C, LANES), 1)
    col = lane & (RWKV_HEAD - 1)
    head0 = lane < RWKV_HEAD
    strict = col < row
    incl = col <= row
    eye = (col == row).astype(F32)
    blk = row ^ col
    r128 = lax.broadcasted_iota(jnp.int32, (LANES, LANES), 0)
    c128 = lax.broadcasted_iota(jnp.int32, (LANES, LANES), 1)
    same_head = (r128 < RWKV_HEAD) == (c128 < RWKV_HEAD)
    ones_blk = same_head.astype(BF16)
    tril = (lax.broadcasted_iota(jnp.int32, (C, C), 0)
            >= lax.broadcasted_iota(jnp.int32, (C, C), 1)).astype(BF16)
    rowq = lax.broadcasted_iota(jnp.int32, (C, N_RW_COLS), 0)

    def bd(y):
        return jnp.concatenate([jnp.where(head0, y, 0.0), jnp.where(head0, 0.0, y)], axis=0)

    def bmm(x, y):
        return _dot(x, bd(y))

    def headsum(x):
        return jnp.concatenate(
            [jnp.dot(x[:, p * LANES:(p + 1) * LANES].astype(BF16), ones_blk,
                     preferred_element_type=F32) for p in range(RW_PAIRS)], axis=-1)

    nt = (((1,), (1,)), ((), ()))
    tn = (((0,), (0,)), ((), ()))

    for bi in range(Bt):
        q = q_ref[bi]
        prev = jnp.where(rowq == 0, prev_scr[bi], pltpu.roll(q, 1, 0))
        prev_scr[bi] = q[C - 1:C, :]
        qs = q + mu_ref[...] * (prev - q)
        r = qs[:, 0:D_RWKV]
        k = qs[:, D_RWKV:2 * D_RWKV]
        v = qs[:, 2 * D_RWKV:3 * D_RWKV]
        xwa = qs[:, 3 * D_RWKV:3 * D_RWKV + LANES]
        xg = qs[:, 3 * D_RWKV + LANES:]
        wa = _dot(jnp.where(head0, jnp.tanh(xwa), xwa), wwa_ref[...])
        lw = -math.exp(-0.5) * jax.nn.sigmoid(w0_ref[...] + wa[:, :D_RWKV])
        a = jax.nn.sigmoid(a0_ref[...] + wa[:, D_RWKV:])
        g = _dot(jax.nn.sigmoid(xg), gup_ref[...])
        kk = k * kk_ref[...]
        kk = kk / jnp.maximum(jnp.sqrt(headsum(kk * kk)), 1e-12)
        k2 = k * (1.0 + (a - 1.0) * ka_ref[...])
        bonus = headsum(r * k2 * rk_ref[...]) * v

        lw_hi = lw.astype(BF16)
        lw_lo = (lw - lw_hi.astype(F32)).astype(BF16)
        cum = (jnp.dot(tril, lw_hi, preferred_element_type=F32)
               + jnp.dot(tril, lw_lo, preferred_element_type=F32))
        w_in = jnp.exp(cum)
        w_ex = jnp.exp(cum - lw)
        w_inv = jnp.exp(-cum)
        ah = -kk * w_ex
        bh = kk * a * w_inv
        kh = k2 * w_inv
        rh = r * w_in
        w_end = w_in[C - 1:C, :]

        outs = []
        for p in range(RW_PAIRS):
            sl = slice(p * LANES, (p + 1) * LANES)
            ahp, bhp, khp, rhp, vp, wep = ah[:, sl], bh[:, sl], kh[:, sl], rh[:, sl], v[:, sl], w_end[:, sl]
            S = s_scr[bi, p]
            lhs = jnp.concatenate([ahp, rhp], axis=0).astype(BF16)
            rhs = jnp.concatenate([bd(bhp), bd(khp)], axis=0).astype(BF16)
            gram = lax.dot_general(lhs, rhs, nt, preferred_element_type=F32)
            a_ab = jnp.where(strict, gram[:C, :LANES], 0.0)
            a_ak = jnp.where(strict, gram[:C, LANES:], 0.0)
            r_b = jnp.where(incl, gram[C:, :LANES], 0.0)
            r_k = jnp.where(incl, gram[C:, LANES:], 0.0)
            pm = jnp.where(blk < 8, a_ab, 0.0)
            p2 = bmm(pm, pm)
            p4 = bmm(p2, p2)
            ip = eye + pm
            t_inv = ip + bmm(ip, p2)
            t_inv = t_inv + bmm(t_inv, p4)
            gsz = 8
            while gsz < C:
                m = jnp.where((blk >= gsz) & (blk < 2 * gsz), a_ab, 0.0)
                t_inv = t_inv + bmm(bmm(t_inv, m), t_inv)
                gsz *= 2
            ss = lax.dot_general(lhs, S.astype(BF16), nt, preferred_element_type=F32)
            x = ss[:C] + bmm(a_ak, vp)
            u = bmm(t_inv, x)
            o = ss[C:] + _dot(jnp.concatenate([r_b, r_k], axis=1),
                              jnp.concatenate([bd(u), bd(vp)], axis=0))
            upd = lax.dot_general(
                jnp.concatenate([u, vp], axis=0).astype(BF16),
                jnp.concatenate([bhp * wep, khp * wep], axis=0).astype(BF16),
                tn, preferred_element_type=F32)
            s_scr[bi, p] = S * wep + jnp.where(same_head, upd, 0.0)
            outs.append(o)
        o = jnp.concatenate(outs, axis=-1)
        mean = headsum(o) * (1.0 / RWKV_HEAD)
        dlt = o - mean
        var = headsum(dlt * dlt) * (1.0 / RWKV_HEAD)
        on = dlt * lax.rsqrt(var + GN_EPS) * gnw_ref[...] + gnb_ref[...]
        o_ref[bi] = (on + bonus) * g


def _rwkv(q, mu, w0, wwa, a0, g_up, k_k, k_a, r_k, gn_w, gn_b, *, Bt=1):
    B, L, _ = q.shape
    C = RW_CHUNK
    const = lambda *shape: pl.BlockSpec(shape, lambda b, c: (0,) * len(shape))
    return pl.pallas_call(
        functools.partial(_rwkv_kernel, C=C, Bt=Bt),
        grid=(B // Bt, L // C),
        in_specs=[
            pl.BlockSpec((Bt, C, N_RW_COLS), lambda b, c: (b, c, 0)),
            const(1, N_RW_COLS),
            const(1, D_RWKV),
            const(LANES, 2 * D_RWKV),
            const(1, D_RWKV),
            const(G_RANK, D_RWKV),
            const(1, D_RWKV),
            const(1, D_RWKV),
            const(1, D_RWKV),
            const(1, D_RWKV),
            const(1, D_RWKV),
        ],
        out_specs=pl.BlockSpec((Bt, C, D_RWKV), lambda b, c: (b, c, 0)),
        out_shape=jax.ShapeDtypeStruct((B, L, D_RWKV), F32),
        scratch_shapes=[
            pltpu.VMEM((Bt, RW_PAIRS, LANES, LANES), F32),
            pltpu.VMEM((Bt, 1, N_RW_COLS), F32),
        ],
        compiler_params=pltpu.CompilerParams(
            dimension_semantics=("parallel", "arbitrary"), vmem_limit_bytes=VMEM_LIMIT),
        name="rwkv7",
    )(q, mu, w0, wwa, a0, g_up, k_k, k_a, r_k, gn_w, gn_b)


def _ffn_kernel(ys_ref, yr_ref, x_ref, wo_ref, gpm_ref, gpre_ref, wup_ref, wdn_ref, gpost_ref,
                o_ref):
    mix = _dot(ys_ref[...], wo_ref[:D_S5, :]) + _dot(yr_ref[0], wo_ref[D_S5:, :])
    h = x_ref[0] + _rms(mix, gpm_ref[...])
    hn = _rms(h, gpre_ref[...])
    up = _dot(hn, wup_ref[...])
    ff = _dot(jnp.square(jnp.maximum(up, 0.0)), wdn_ref[...])
    o_ref[0] = h + _rms(ff, gpost_ref[...])


def _ffn(ys_tm, yr, x, w_out, g_post_mix, g_pre_mlp, w_up, w_down, g_post_mlp):
    B, L, _ = x.shape
    tl = min(ROW_TILE, L)
    const = lambda *shape: pl.BlockSpec(shape, lambda b, i: (0,) * len(shape),
                                        pipeline_mode=pl.Buffered(1))
    return pl.pallas_call(
        _ffn_kernel,
        grid=(B, L // tl),
        in_specs=[
            pl.BlockSpec((tl, D_S5), lambda b, i: (i, b)),
            pl.BlockSpec((1, tl, D_RWKV), lambda b, i: (b, i, 0)),
            pl.BlockSpec((1, tl, D_MODEL), lambda b, i: (b, i, 0)),
            const(D_MODEL, D_MODEL),
            const(1, D_MODEL),
            const(1, D_MODEL),
            const(D_MODEL, D_FF),
            const(D_FF, D_MODEL),
            const(1, D_MODEL),
        ],
        out_specs=pl.BlockSpec((1, tl, D_MODEL), lambda b, i: (b, i, 0)),
        out_shape=jax.ShapeDtypeStruct((B, L, D_MODEL), F32),
        compiler_params=pltpu.CompilerParams(
            dimension_semantics=("parallel", "parallel"), vmem_limit_bytes=VMEM_LIMIT),
        name="ffn",
    )(ys_tm.reshape(L, B * D_S5), yr, x, w_out, g_post_mix, g_pre_mlp, w_up, w_down, g_post_mlp)


def _layer(x, g_pre_mix, w_in, s5_lam_re, s5_lam_im, s5_log_dt, s5_b_re, s5_b_im, s5_c_re, s5_c_im,
           s5_d, s5_w_glu, s5_b_glu, rw_mu, rw_w0, rw_w_up, rw_a0, rw_a_up, rw_g_up, rw_k_k, rw_k_a,
           rw_r_k, rw_gn_w, rw_gn_b, w_out, g_post_mix, g_pre_mlp, w_ff_up, w_ff_down, g_post_mlp):
    row = lambda t: t.reshape(1, -1)
    u_tm, q = _inproj(x, row(g_pre_mix), w_in.astype(BF16))
    a_re, a_im, b2, c2 = _s5_params(s5_lam_re, s5_lam_im, s5_log_dt, s5_b_re, s5_b_im,
                                    s5_c_re, s5_c_im)
    ys_tm = _s5(u_tm, a_re, a_im, b2, c2, row(s5_d), s5_w_glu.astype(BF16), row(s5_b_glu))
    zeros = jnp.zeros((W_RANK, D_RWKV), F32)
    wwa = jnp.concatenate([jnp.concatenate([rw_w_up, zeros], axis=1),
                           jnp.concatenate([zeros, rw_a_up], axis=1)], axis=0).astype(BF16)
    yr = _rwkv(q, row(rw_mu), row(rw_w0), wwa, row(rw_a0), rw_g_up.astype(BF16), row(rw_k_k),
               row(rw_k_a), row(rw_r_k), row(rw_gn_w), row(rw_gn_b))
    return _ffn(ys_tm, yr, x, w_out.astype(BF16), row(g_post_mix), row(g_pre_mlp),
                w_ff_up.astype(BF16), w_ff_down.astype(BF16), row(g_post_mlp))


def kernel(x, g_pre_mix, w_in, s5_lam_re, s5_lam_im, s5_log_dt, s5_b_re, s5_b_im, s5_c_re, s5_c_im,
           s5_d, s5_w_glu, s5_b_glu, rw_mu, rw_w0, rw_w_up, rw_a0, rw_a_up, rw_g_up, rw_k_k, rw_k_a,
           rw_r_k, rw_gn_w, rw_gn_b, w_out, g_post_mix, g_pre_mlp, w_ff_up, w_ff_down, g_post_mlp):
    params = (g_pre_mix, w_in, s5_lam_re, s5_lam_im, s5_log_dt, s5_b_re, s5_b_im, s5_c_re, s5_c_im,
              s5_d, s5_w_glu, s5_b_glu, rw_mu, rw_w0, rw_w_up, rw_a0, rw_a_up, rw_g_up, rw_k_k,
              rw_k_a, rw_r_k, rw_gn_w, rw_gn_b, w_out, g_post_mix, g_pre_mlp, w_ff_up, w_ff_down,
              g_post_mlp)
    h = x
    for layer in range(g_pre_mix.shape[0]):
        h = _layer(h, *[p[layer] for p in params])
    return h
```

```python
import functools
import math

import jax
import jax.numpy as jnp
from jax import lax
from jax.experimental import pallas as pl
from jax.experimental.pallas import tpu as pltpu

F32 = jnp.float32
BF16 = jnp.bfloat16

D_MODEL = 1024
D_S5 = 512
D_RWKV = 512
S5_GROUP = 16
S5_GROUPS = 32
S5_STATE = 64
RWKV_HEAD = 64
RWKV_HEADS = 8
W_RANK = 64
A_RANK = 64
G_RANK = 128
N_RW_COLS = 3 * D_RWKV + W_RANK + A_RANK + G_RANK
N_PROJ = D_S5 + N_RW_COLS
D_FF = 4 * D_MODEL
RMS_EPS = 1e-6
GN_EPS = 64e-5

SUBLANES = 8
LANES = 128

ROW_TILE = 512
S5_BATCH = SUBLANES
S5_T = 64
S5_HALF_GROUPS = S5_GROUPS // 2
S5_HALF_IN = S5_HALF_GROUPS * S5_GROUP
S5_HALF_STATE = S5_HALF_GROUPS * S5_STATE
S5_SCAN_LANES = 512
RW_CHUNK = 64
RW_PAIRS = RWKV_HEADS // 2
RW_BT = 4
VMEM_LIMIT = 56 * 1024 * 1024


def _rms(x, g):
    return x * lax.rsqrt(jnp.mean(x * x, axis=-1, keepdims=True) + RMS_EPS) * g


def _dot(a, b):
    return jnp.dot(a.astype(BF16), b.astype(BF16), preferred_element_type=F32)


def _inproj_kernel(x_ref, g_ref, w_ref, u_ref, q_ref):
    xn = _rms(x_ref[0], g_ref[...])
    p = _dot(xn, w_ref[...])
    u_ref[...] = p[:, :D_S5]
    q_ref[0] = p[:, D_S5:]


def _inproj(x, g, w_in):
    B, L, _ = x.shape
    tl = min(ROW_TILE, L)
    u, q = pl.pallas_call(
        _inproj_kernel,
        grid=(B, L // tl),
        in_specs=[
            pl.BlockSpec((1, tl, D_MODEL), lambda b, i: (b, i, 0)),
            pl.BlockSpec((1, D_MODEL), lambda b, i: (0, 0)),
            pl.BlockSpec((D_MODEL, N_PROJ), lambda b, i: (0, 0)),
        ],
        out_specs=[
            pl.BlockSpec((tl, D_S5), lambda b, i: (i, b)),
            pl.BlockSpec((1, tl, N_RW_COLS), lambda b, i: (b, i, 0)),
        ],
        out_shape=[
            jax.ShapeDtypeStruct((L, B * D_S5), F32),
            jax.ShapeDtypeStruct((B, L, N_RW_COLS), F32),
        ],
        compiler_params=pltpu.CompilerParams(
            dimension_semantics=("parallel", "parallel"), vmem_limit_bytes=VMEM_LIMIT),
        name="inproj",
    )(x, g, w_in)
    return u.reshape(L, B, D_S5), q


def _s5_kernel(u_ref, are_ref, aim_ref, b2_ref, c2_ref, d_ref, wglu_ref, bglu_ref,
               y_ref, s_scr, bu_scr, *, T):
    rows = T * S5_BATCH
    half_cols = 2 * S5_HALF_STATE

    @pl.when(pl.program_id(1) == 0)
    def _():
        s_scr[...] = jnp.zeros_like(s_scr)

    u = u_ref[...].reshape(rows, D_S5)
    for h in range(2):
        bu_scr[:, h * half_cols:(h + 1) * half_cols] = _dot(
            u[:, h * S5_HALF_IN:(h + 1) * S5_HALF_IN], b2_ref[h])

    for h in range(2):
        for j in range(S5_HALF_STATE // S5_SCAN_LANES):
            re0 = h * half_cols + j * S5_SCAN_LANES
            im0 = re0 + S5_HALF_STATE
            a0 = h * S5_HALF_STATE + j * S5_SCAN_LANES
            are = jnp.broadcast_to(are_ref[:, a0:a0 + S5_SCAN_LANES], (S5_BATCH, S5_SCAN_LANES))
            aim = jnp.broadcast_to(aim_ref[:, a0:a0 + S5_SCAN_LANES], (S5_BATCH, S5_SCAN_LANES))

            def body(t, carry, re0=re0, im0=im0, are=are, aim=aim):
                sre, sim = carry
                r0 = pl.multiple_of(t * S5_BATCH, S5_BATCH)
                bre = bu_scr[pl.ds(r0, S5_BATCH), re0:re0 + S5_SCAN_LANES]
                bim = bu_scr[pl.ds(r0, S5_BATCH), im0:im0 + S5_SCAN_LANES]
                nre = are * sre - aim * sim + bre
                nim = are * sim + aim * sre + bim
                bu_scr[pl.ds(r0, S5_BATCH), re0:re0 + S5_SCAN_LANES] = nre
                bu_scr[pl.ds(r0, S5_BATCH), im0:im0 + S5_SCAN_LANES] = nim
                return nre, nim

            sre, sim = lax.fori_loop(
                0, T, body,
                (s_scr[:, re0:re0 + S5_SCAN_LANES], s_scr[:, im0:im0 + S5_SCAN_LANES]),
                unroll=8)
            s_scr[:, re0:re0 + S5_SCAN_LANES] = sre
            s_scr[:, im0:im0 + S5_SCAN_LANES] = sim

    y = jnp.concatenate(
        [_dot(bu_scr[:, h * half_cols:(h + 1) * half_cols], c2_ref[h]) for h in range(2)],
        axis=-1)
    y = y + d_ref[...] * u
    z = jax.nn.gelu(y)
    gate = jax.nn.sigmoid(_dot(z, wglu_ref[...]) + bglu_ref[...])
    y_ref[...] = (z * gate).reshape(T, S5_BATCH, D_S5)


def _s5_params(lam_re, lam_im, log_dt, b_re, b_im, c_re, c_im):
    dt = jnp.exp(log_dt)[:, None]
    mag = jnp.exp(lam_re * dt)
    ab_re = mag * jnp.cos(lam_im * dt)
    ab_im = mag * jnp.sin(lam_im * dt)
    den = lam_re * lam_re + lam_im * lam_im
    nr = ab_re - 1.0
    f_re = (nr * lam_re + ab_im * lam_im) / den
    f_im = (ab_im * lam_re - nr * lam_im) / den
    bb_re = f_re[..., None] * b_re - f_im[..., None] * b_im
    bb_im = f_re[..., None] * b_im + f_im[..., None] * b_re
    eye = jnp.eye(S5_HALF_GROUPS, dtype=F32)

    def half(t, h):
        return t[h * S5_HALF_GROUPS:(h + 1) * S5_HALF_GROUPS]

    def b_block(bb, h):
        return jnp.einsum("gpc,gk->gckp", half(bb, h), eye).reshape(S5_HALF_IN, S5_HALF_STATE)

    def c_block(cc, h):
        return jnp.einsum("gcp,gk->gpkc", half(cc, h), eye).reshape(S5_HALF_STATE, S5_HALF_IN)

    b2 = jnp.stack([jnp.concatenate([b_block(bb_re, h), b_block(bb_im, h)], axis=1)
                    for h in range(2)]).astype(BF16)
    c2 = jnp.stack([jnp.concatenate([c_block(c_re, h), -c_block(c_im, h)], axis=0)
                    for h in range(2)]).astype(BF16)
    return ab_re.reshape(1, -1), ab_im.reshape(1, -1), b2, c2


def _s5(u_tm, a_re, a_im, b2, c2, d_skip, w_glu, b_glu):
    L, B, _ = u_tm.shape
    T = min(S5_T, L)
    n_state = 2 * 2 * S5_HALF_STATE
    const = lambda *shape: pl.BlockSpec(shape, lambda b, t: (0,) * len(shape))
    return pl.pallas_call(
        functools.partial(_s5_kernel, T=T),
        grid=(B // S5_BATCH, L // T),
        in_specs=[
            pl.BlockSpec((T, S5_BATCH, D_S5), lambda b, t: (t, b, 0)),
            const(1, S5_GROUPS * S5_STATE),
            const(1, S5_GROUPS * S5_STATE),
            const(2, S5_HALF_IN, 2 * S5_HALF_STATE),
            const(2, 2 * S5_HALF_STATE, S5_HALF_IN),
            const(1, D_S5),
            const(D_S5, D_S5),
            const(1, D_S5),
        ],
        out_specs=pl.BlockSpec((T, S5_BATCH, D_S5), lambda b, t: (t, b, 0)),
        out_shape=jax.ShapeDtypeStruct((L, B, D_S5), F32),
        scratch_shapes=[
            pltpu.VMEM((S5_BATCH, n_state), F32),
            pltpu.VMEM((T * S5_BATCH, n_state), F32),
        ],
        compiler_params=pltpu.CompilerParams(
            dimension_semantics=("parallel", "arbitrary"), vmem_limit_bytes=VMEM_LIMIT),
        name="s5",
    )(u_tm, a_re, a_im, b2, c2, d_skip, w_glu, b_glu)


def _rwkv_kernel(q_ref, mu_ref, w0_ref, wwa_ref, a0_ref, gup_ref, kk_ref, ka_ref, rk_ref,
                 gnw_ref, gnb_ref, o_ref, s_scr, prev_scr, *, C, Bt):
    @pl.when(pl.program_id(1) == 0)
    def _():
        s_scr[...] = jnp.zeros_like(s_scr)
        prev_scr[...] = jnp.zeros_like(prev_scr)

    row = lax.broadcasted_iota(jnp.int32, (C, LANES), 0)
    lane = lax.broadcasted_iota(jnp.int32, (C, LANES), 1)
    col = lane & (RWKV_HEAD - 1)
    head0 = lane < RWKV_HEAD
    strict = col < row
    incl = col <= row
    eye = (col == row).astype(F32)
    blk = row ^ col
    r128 = lax.broadcasted_iota(jnp.int32, (LANES, LANES), 0)
    c128 = lax.broadcasted_iota(jnp.int32, (LANES, LANES), 1)
    same_head = (r128 < RWKV_HEAD) == (c128 < RWKV_HEAD)
    ones_blk = same_head.astype(BF16)
    tril = (lax.broadcasted_iota(jnp.int32, (C, C), 0)
            >= lax.broadcasted_iota(jnp.int32, (C, C), 1)).astype(BF16)
    rows = Bt * C
    rowq = lax.broadcasted_iota(jnp.int32, (rows, N_RW_COLS), 0)
    sl = [slice(p * LANES, (p + 1) * LANES) for p in range(RW_PAIRS)]
    bis = list(range(Bt))

    def bd(y):
        return jnp.concatenate([jnp.where(head0, y, 0.0), jnp.where(head0, 0.0, y)], axis=0)

    def bmm(x, y):
        return _dot(x, bd(y))

    def headsum(x):
        n = x.shape[0]
        tall = jnp.concatenate([x[:, s_] for s_ in sl], axis=0)
        t = jnp.dot(tall.astype(BF16), ones_blk, preferred_element_type=F32)
        return jnp.concatenate([t[p * n:(p + 1) * n] for p in range(RW_PAIRS)], axis=-1)

    nt = (((1,), (1,)), ((), ()))
    tn = (((0,), (0,)), ((), ()))

    def each(fn, *lists):
        return [fn(*args) for args in zip(*lists)]

    q = q_ref[...].reshape(rows, N_RW_COLS)
    prev = pltpu.roll(q, 1, 0)
    for bi in bis:
        prev = jnp.where(rowq == bi * C, prev_scr[bi], prev)
        prev_scr[bi] = q[(bi + 1) * C - 1:(bi + 1) * C, :]
    qs = q + mu_ref[...] * (prev - q)
    r = qs[:, 0:D_RWKV]
    k = qs[:, D_RWKV:2 * D_RWKV]
    v = qs[:, 2 * D_RWKV:3 * D_RWKV]
    xwa = qs[:, 3 * D_RWKV:3 * D_RWKV + LANES]
    head0r = lax.broadcasted_iota(jnp.int32, (rows, LANES), 1) < RWKV_HEAD
    wa = _dot(jnp.where(head0r, jnp.tanh(xwa), xwa), wwa_ref[...])
    g = _dot(jax.nn.sigmoid(qs[:, 3 * D_RWKV + LANES:]), gup_ref[...])
    lw = -math.exp(-0.5) * jax.nn.sigmoid(w0_ref[...] + wa[:, :D_RWKV])
    a = jax.nn.sigmoid(a0_ref[...] + wa[:, D_RWKV:])
    kk = k * kk_ref[...]
    kk = kk / jnp.maximum(jnp.sqrt(headsum(kk * kk)), 1e-12)
    k2 = k * (1.0 + (a - 1.0) * ka_ref[...])
    bonus = headsum(r * k2 * rk_ref[...]) * v

    lw_hi = lw.astype(BF16)
    lw_lo = (lw - lw_hi.astype(F32)).astype(BF16)
    cum = jnp.concatenate(
        [jnp.dot(tril, lw_hi[bi * C:(bi + 1) * C], preferred_element_type=F32)
         + jnp.dot(tril, lw_lo[bi * C:(bi + 1) * C], preferred_element_type=F32) for bi in bis],
        axis=0)
    w_in = jnp.exp(cum)
    w_inv = jnp.exp(-cum)
    ah = -kk * jnp.exp(cum - lw)
    bh = kk * a * w_inv
    kh = k2 * w_inv
    rh = r * w_in

    ch = [(bi, p) for bi in bis for p in range(RW_PAIRS)]
    pick = lambda t: [t[bi * C:(bi + 1) * C, sl[p]] for bi, p in ch]
    ahp, bhp, khp, rhp, vp = pick(ah), pick(bh), pick(kh), pick(rh), pick(v)
    wep = [w_in[(bi + 1) * C - 1:(bi + 1) * C, sl[p]] for bi, p in ch]
    S = [s_scr[bi, p] for bi, p in ch]
    lhs = each(lambda x, y: jnp.concatenate([x, y], axis=0).astype(BF16), ahp, rhp)
    rhs = each(lambda x, y: jnp.concatenate([bd(x), bd(y)], axis=0).astype(BF16), bhp, khp)
    gram = each(lambda x, y: lax.dot_general(x, y, nt, preferred_element_type=F32), lhs, rhs)
    a_ab = [jnp.where(strict, t[:C, :LANES], 0.0) for t in gram]
    a_ak = [jnp.where(strict, t[:C, LANES:], 0.0) for t in gram]
    r_b = [jnp.where(incl, t[C:, :LANES], 0.0) for t in gram]
    r_k = [jnp.where(incl, t[C:, LANES:], 0.0) for t in gram]
    ss = each(lambda x, st: lax.dot_general(x, st.astype(BF16), nt, preferred_element_type=F32),
              lhs, S)
    pm = [jnp.where(blk < 8, t, 0.0) for t in a_ab]
    p2 = each(bmm, pm, pm)
    p4 = each(bmm, p2, p2)
    ip = [eye + t for t in pm]
    t_inv = each(lambda x, y: x + bmm(x, y), ip, p2)
    t_inv = each(lambda x, y: x + bmm(x, y), t_inv, p4)
    gsz = 8
    while gsz < C:
        m = [jnp.where((blk >= gsz) & (blk < 2 * gsz), t, 0.0) for t in a_ab]
        tm = each(bmm, t_inv, m)
        t_inv = each(lambda x, y: x + bmm(y, x), t_inv, tm)
        gsz *= 2
    x = each(lambda s_, aa, vv: s_[:C] + bmm(aa, vv), ss, a_ak, vp)
    u = each(bmm, t_inv, x)
    o = each(lambda s_, rb, rk_, uu, vv: s_[C:] + _dot(jnp.concatenate([rb, rk_], axis=1),
                                                       jnp.concatenate([bd(uu), bd(vv)], axis=0)),
             ss, r_b, r_k, u, vp)
    upd = each(lambda uu, vv, bb, kx, we: lax.dot_general(
        jnp.concatenate([uu, vv], axis=0).astype(BF16),
        jnp.concatenate([bb * we, kx * we], axis=0).astype(BF16),
        tn, preferred_element_type=F32), u, vp, bhp, khp, wep)
    for (bi, p), st, we, up in zip(ch, S, wep, upd):
        s_scr[bi, p] = st * we + jnp.where(same_head, up, 0.0)
    o = jnp.concatenate(
        [jnp.concatenate(o[bi * RW_PAIRS:(bi + 1) * RW_PAIRS], axis=-1) for bi in bis], axis=0)
    dlt = o - headsum(o) * (1.0 / RWKV_HEAD)
    var = headsum(dlt * dlt) * (1.0 / RWKV_HEAD)
    on = dlt * lax.rsqrt(var + GN_EPS) * gnw_ref[...] + gnb_ref[...]
    o_ref[...] = ((on + bonus) * g).reshape(Bt, C, D_RWKV)


def _rwkv(q, mu, w0, wwa, a0, g_up, k_k, k_a, r_k, gn_w, gn_b, *, Bt=RW_BT):
    B, L, _ = q.shape
    C = RW_CHUNK
    const = lambda *shape: pl.BlockSpec(shape, lambda b, c: (0,) * len(shape))
    return pl.pallas_call(
        functools.partial(_rwkv_kernel, C=C, Bt=Bt),
        grid=(B // Bt, L // C),
        in_specs=[
            pl.BlockSpec((Bt, C, N_RW_COLS), lambda b, c: (b, c, 0)),
            const(1, N_RW_COLS),
            const(1, D_RWKV),
            const(LANES, 2 * D_RWKV),
            const(1, D_RWKV),
            const(G_RANK, D_RWKV),
            const(1, D_RWKV),
            const(1, D_RWKV),
            const(1, D_RWKV),
            const(1, D_RWKV),
            const(1, D_RWKV),
        ],
        out_specs=pl.BlockSpec((Bt, C, D_RWKV), lambda b, c: (b, c, 0)),
        out_shape=jax.ShapeDtypeStruct((B, L, D_RWKV), F32),
        scratch_shapes=[
            pltpu.VMEM((Bt, RW_PAIRS, LANES, LANES), F32),
            pltpu.VMEM((Bt, 1, N_RW_COLS), F32),
        ],
        compiler_params=pltpu.CompilerParams(
            dimension_semantics=("parallel", "arbitrary"), vmem_limit_bytes=VMEM_LIMIT),
        name="rwkv7",
    )(q, mu, w0, wwa, a0, g_up, k_k, k_a, r_k, gn_w, gn_b)


def _ffn_kernel(ys_ref, yr_ref, x_ref, wo_ref, gpm_ref, gpre_ref, wup_ref, wdn_ref, gpost_ref,
                o_ref):
    mix = _dot(ys_ref[...], wo_ref[:D_S5, :]) + _dot(yr_ref[0], wo_ref[D_S5:, :])
    h = x_ref[0] + _rms(mix, gpm_ref[...])
    hn = _rms(h, gpre_ref[...])
    up = _dot(hn, wup_ref[...])
    ff = _dot(jnp.square(jnp.maximum(up, 0.0)), wdn_ref[...])
    o_ref[0] = h + _rms(ff, gpost_ref[...])


def _ffn(ys_tm, yr, x, w_out, g_post_mix, g_pre_mlp, w_up, w_down, g_post_mlp):
    B, L, _ = x.shape
    tl = min(ROW_TILE, L)
    const = lambda *shape: pl.BlockSpec(shape, lambda b, i: (0,) * len(shape),
                                        pipeline_mode=pl.Buffered(1))
    return pl.pallas_call(
        _ffn_kernel,
        grid=(B, L // tl),
        in_specs=[
            pl.BlockSpec((tl, D_S5), lambda b, i: (i, b)),
            pl.BlockSpec((1, tl, D_RWKV), lambda b, i: (b, i, 0)),
            pl.BlockSpec((1, tl, D_MODEL), lambda b, i: (b, i, 0)),
            const(D_MODEL, D_MODEL),
            const(1, D_MODEL),
            const(1, D_MODEL),
            const(D_MODEL, D_FF),
            const(D_FF, D_MODEL),
            const(1, D_MODEL),
        ],
        out_specs=pl.BlockSpec((1, tl, D_MODEL), lambda b, i: (b, i, 0)),
        out_shape=jax.ShapeDtypeStruct((B, L, D_MODEL), F32),
        compiler_params=pltpu.CompilerParams(
            dimension_semantics=("parallel", "parallel"), vmem_limit_bytes=VMEM_LIMIT),
        name="ffn",
    )(ys_tm.reshape(L, B * D_S5), yr, x, w_out, g_post_mix, g_pre_mlp, w_up, w_down, g_post_mlp)


def _layer(x, g_pre_mix, w_in, s5_lam_re, s5_lam_im, s5_log_dt, s5_b_re, s5_b_im, s5_c_re, s5_c_im,
           s5_d, s5_w_glu, s5_b_glu, rw_mu, rw_w0, rw_w_up, rw_a0, rw_a_up, rw_g_up, rw_k_k, rw_k_a,
           rw_r_k, rw_gn_w, rw_gn_b, w_out, g_post_mix, g_pre_mlp, w_ff_up, w_ff_down, g_post_mlp):
    row = lambda t: t.reshape(1, -1)
    u_tm, q = _inproj(x, row(g_pre_mix), w_in.astype(BF16))
    a_re, a_im, b2, c2 = _s5_params(s5_lam_re, s5_lam_im, s5_log_dt, s5_b_re, s5_b_im,
                                    s5_c_re, s5_c_im)
    ys_tm = _s5(u_tm, a_re, a_im, b2, c2, row(s5_d), s5_w_glu.astype(BF16), row(s5_b_glu))
    zeros = jnp.zeros((W_RANK, D_RWKV), F32)
    wwa = jnp.concatenate([jnp.concatenate([rw_w_up, zeros], axis=1),
                           jnp.concatenate([zeros, rw_a_up], axis=1)], axis=0).astype(BF16)
    yr = _rwkv(q, row(rw_mu), row(rw_w0), wwa, row(rw_a0), rw_g_up.astype(BF16), row(rw_k_k),
               row(rw_k_a), row(rw_r_k), row(rw_gn_w), row(rw_gn_b))
    return _ffn(ys_tm, yr, x, w_out.astype(BF16), row(g_post_mix), row(g_pre_mlp),
                w_ff_up.astype(BF16), w_ff_down.astype(BF16), row(g_post_mlp))


def kernel(x, g_pre_mix, w_in, s5_lam_re, s5_lam_im, s5_log_dt, s5_b_re, s5_b_im, s5_c_re, s5_c_im,
           s5_d, s5_w_glu, s5_b_glu, rw_mu, rw_w0, rw_w_up, rw_a0, rw_a_up, rw_g_up, rw_k_k, rw_k_a,
           rw_r_k, rw_gn_w, rw_gn_b, w_out, g_post_mix, g_pre_mlp, w_ff_up, w_ff_down, g_post_mlp):
    params = (g_pre_mix, w_in, s5_lam_re, s5_lam_im, s5_log_dt, s5_b_re, s5_b_im, s5_c_re, s5_c_im,
              s5_d, s5_w_glu, s5_b_glu, rw_mu, rw_w0, rw_w_up, rw_a0, rw_a_up, rw_g_up, rw_k_k,
              rw_k_a, rw_r_k, rw_gn_w, rw_gn_b, w_out, g_post_mix, g_pre_mlp, w_ff_up, w_ff_down,
              g_post_mlp)
    h = x
    for layer in range(g_pre_mix.shape[0]):
        h = _layer(h, *[p[layer] for p in params])
    return h
```

```python
import functools
import math

import jax
import jax.numpy as jnp
from jax import lax
from jax.experimental import pallas as pl
from jax.experimental.pallas import tpu as pltpu

F32 = jnp.float32
BF16 = jnp.bfloat16

D_MODEL = 1024
D_S5 = 512
D_RWKV = 512
S5_GROUP = 16
S5_GROUPS = 32
S5_STATE = 64
RWKV_HEAD = 64
RWKV_HEADS = 8
W_RANK = 64
A_RANK = 64
G_RANK = 128
N_RW_COLS = 3 * D_RWKV + W_RANK + A_RANK + G_RANK
N_PROJ = D_S5 + N_RW_COLS
D_FF = 4 * D_MODEL
RMS_EPS = 1e-6
GN_EPS = 64e-5

SUBLANES = 8
LANES = 128

ROW_TILE = 512
S5_BATCH = SUBLANES
S5_T = 64
S5_GROUPS_PER_STEP = 2
S5_BLOCK_GROUPS = LANES // S5_GROUP
S5_BLOCKS = S5_GROUPS // S5_BLOCK_GROUPS
S5_BLOCK_STATE = S5_BLOCK_GROUPS * S5_STATE
S5_BLOCK_COLS = 2 * S5_BLOCK_STATE
RW_CHUNK = 64
RW_PAIRS = RWKV_HEADS // 2
RW_BT = 4
VMEM_LIMIT = 56 * 1024 * 1024


def _rms(x, g):
    return x * lax.rsqrt(jnp.mean(x * x, axis=-1, keepdims=True) + RMS_EPS) * g


def _dot(a, b):
    return jnp.dot(a.astype(BF16), b.astype(BF16), preferred_element_type=F32)


def _inproj_kernel(x_ref, g_ref, w_ref, u_ref, q_ref):
    xn = _rms(x_ref[0], g_ref[...])
    p = _dot(xn, w_ref[...])
    u_ref[...] = p[:, :D_S5]
    q_ref[0] = p[:, D_S5:]


def _inproj(x, g, w_in):
    B, L, _ = x.shape
    tl = min(ROW_TILE, L)
    u, q = pl.pallas_call(
        _inproj_kernel,
        grid=(B, L // tl),
        in_specs=[
            pl.BlockSpec((1, tl, D_MODEL), lambda b, i: (b, i, 0)),
            pl.BlockSpec((1, D_MODEL), lambda b, i: (0, 0)),
            pl.BlockSpec((D_MODEL, N_PROJ), lambda b, i: (0, 0)),
        ],
        out_specs=[
            pl.BlockSpec((tl, D_S5), lambda b, i: (i, b)),
            pl.BlockSpec((1, tl, N_RW_COLS), lambda b, i: (b, i, 0)),
        ],
        out_shape=[
            jax.ShapeDtypeStruct((L, B * D_S5), F32),
            jax.ShapeDtypeStruct((B, L, N_RW_COLS), F32),
        ],
        compiler_params=pltpu.CompilerParams(
            dimension_semantics=("parallel", "parallel"), vmem_limit_bytes=VMEM_LIMIT),
        name="inproj",
    )(x, g, w_in)
    return u.reshape(L, B, D_S5), q


def _s5_kernel(u_ref, are_ref, aim_ref, b4_ref, c2_ref, d_ref, wglu_ref, bglu_ref,
               y_ref, s_scr, bu_scr, sb_scr, *, T, NG):
    rows = T * S5_BATCH
    pair_cols = 2 * S5_BLOCK_COLS

    @pl.when(pl.program_id(1) == 0)
    def _():
        s_scr[...] = jnp.zeros_like(s_scr)

    us = [u_ref[:, g * S5_BATCH:(g + 1) * S5_BATCH, :].reshape(rows, D_S5) for g in range(NG)]
    ubs = [u.astype(BF16) for u in us]

    def bu(g, h):
        for q in (2 * h, 2 * h + 1):
            bu_scr[g, :, q * S5_BLOCK_COLS:(q + 1) * S5_BLOCK_COLS] = jnp.dot(
                ubs[g][:, q * LANES:(q + 1) * LANES], b4_ref[q], preferred_element_type=F32)

    def scan(g, h):
        qs = (2 * h, 2 * h + 1)
        a_bc = {}
        st = {}
        for q in qs:
            a0 = q * S5_BLOCK_STATE
            a_bc[q] = (
                jnp.broadcast_to(are_ref[:, a0:a0 + S5_BLOCK_STATE], (S5_BATCH, S5_BLOCK_STATE)),
                jnp.broadcast_to(aim_ref[:, a0:a0 + S5_BLOCK_STATE], (S5_BATCH, S5_BLOCK_STATE)))
            re0 = q * S5_BLOCK_COLS
            st[q] = (s_scr[g, :, re0:re0 + S5_BLOCK_STATE],
                     s_scr[g, :, re0 + S5_BLOCK_STATE:re0 + S5_BLOCK_COLS])
        held = {}
        for t in range(T):
            r0 = t * S5_BATCH
            for q in qs:
                re0 = q * S5_BLOCK_COLS
                im0 = re0 + S5_BLOCK_STATE
                are, aim = a_bc[q]
                sre, sim = st[q]
                nre = are * sre - aim * sim + bu_scr[g, r0:r0 + S5_BATCH, re0:re0 + S5_BLOCK_STATE]
                nim = are * sim + aim * sre + bu_scr[g, r0:r0 + S5_BATCH, im0:im0 + S5_BLOCK_STATE]
                st[q] = (nre, nim)
                if t % 2 == 0:
                    held[q] = (nre, nim)
                else:
                    p0 = r0 - S5_BATCH
                    sb_scr[g, p0:p0 + 2 * S5_BATCH, re0:re0 + S5_BLOCK_STATE] = jnp.concatenate(
                        [held[q][0], nre], axis=0).astype(BF16)
                    sb_scr[g, p0:p0 + 2 * S5_BATCH, im0:im0 + S5_BLOCK_STATE] = jnp.concatenate(
                        [held[q][1], nim], axis=0).astype(BF16)
        for q in qs:
            re0 = q * S5_BLOCK_COLS
            s_scr[g, :, re0:re0 + S5_BLOCK_STATE] = st[q][0]
            s_scr[g, :, re0 + S5_BLOCK_STATE:re0 + S5_BLOCK_COLS] = st[q][1]

    def cs(g, h):
        return jnp.dot(sb_scr[g, :, h * pair_cols:(h + 1) * pair_cols], c2_ref[h],
                       preferred_element_type=F32)

    def gelu_in(g, ys):
        return jax.nn.gelu(jnp.concatenate(ys, axis=-1) + d_ref[...] * us[g])

    def glu_out(g, z):
        gate = jax.nn.sigmoid(_dot(z, wglu_ref[...]) + bglu_ref[...])
        y_ref[:, g * S5_BATCH:(g + 1) * S5_BATCH, :] = (z * gate).reshape(T, S5_BATCH, D_S5)

    n_pairs = S5_BLOCKS // 2
    for h in range(n_pairs):
        bu(0, h)
    ys = {g: [] for g in range(NG)}
    for g in range(NG):
        for h in range(n_pairs):
            scan(g, h)
            if g + 1 < NG:
                bu(g + 1, h)
            ys[g].append(cs(g, h))
    zs = [gelu_in(g, ys[g]) for g in range(NG)]
    for g in range(NG):
        glu_out(g, zs[g])


def _s5_params(lam_re, lam_im, log_dt, b_re, b_im, c_re, c_im):
    dt = jnp.exp(log_dt)[:, None]
    mag = jnp.exp(lam_re * dt)
    ab_re = mag * jnp.cos(lam_im * dt)
    ab_im = mag * jnp.sin(lam_im * dt)
    den = lam_re * lam_re + lam_im * lam_im
    nr = ab_re - 1.0
    f_re = (nr * lam_re + ab_im * lam_im) / den
    f_im = (ab_im * lam_re - nr * lam_im) / den
    bb_re = f_re[..., None] * b_re - f_im[..., None] * b_im
    bb_im = f_re[..., None] * b_im + f_im[..., None] * b_re
    eye = jnp.eye(S5_BLOCK_GROUPS, dtype=F32)

    def blk(t, q):
        return t[q * S5_BLOCK_GROUPS:(q + 1) * S5_BLOCK_GROUPS]

    def b_block(bb, q):
        return jnp.einsum("gpc,gk->gckp", blk(bb, q), eye).reshape(LANES, S5_BLOCK_STATE)

    def c_block(cc, q):
        return jnp.einsum("gcp,gk->gpkc", blk(cc, q), eye).reshape(S5_BLOCK_STATE, LANES)

    b4 = jnp.stack([jnp.concatenate([b_block(bb_re, q), b_block(bb_im, q)], axis=1)
                    for q in range(S5_BLOCKS)]).astype(BF16)
    c4 = [jnp.concatenate([c_block(c_re, q), -c_block(c_im, q)], axis=0)
          for q in range(S5_BLOCKS)]
    zc = jnp.zeros_like(c4[0])
    c2 = jnp.stack([jnp.concatenate([jnp.concatenate([c4[2 * h], zc], axis=1),
                                     jnp.concatenate([zc, c4[2 * h + 1]], axis=1)], axis=0)
                    for h in range(S5_BLOCKS // 2)]).astype(BF16)
    return ab_re.reshape(1, -1), ab_im.reshape(1, -1), b4, c2


def _s5(u_tm, a_re, a_im, b4, c2, d_skip, w_glu, b_glu):
    L, B, _ = u_tm.shape
    T = min(S5_T, L)
    NG = min(S5_GROUPS_PER_STEP, B // S5_BATCH)
    n_state = S5_BLOCKS * S5_BLOCK_COLS
    const = lambda *shape: pl.BlockSpec(shape, lambda b, t: (0,) * len(shape))
    return pl.pallas_call(
        functools.partial(_s5_kernel, T=T, NG=NG),
        grid=(B // (NG * S5_BATCH), L // T),
        in_specs=[
            pl.BlockSpec((T, NG * S5_BATCH, D_S5), lambda b, t: (t, b, 0)),
            const(1, S5_GROUPS * S5_STATE),
            const(1, S5_GROUPS * S5_STATE),
            const(S5_BLOCKS, LANES, S5_BLOCK_COLS),
            const(S5_BLOCKS // 2, 2 * S5_BLOCK_COLS, 2 * LANES),
            const(1, D_S5),
            const(D_S5, D_S5),
            const(1, D_S5),
        ],
        out_specs=pl.BlockSpec((T, NG * S5_BATCH, D_S5), lambda b, t: (t, b, 0)),
        out_shape=jax.ShapeDtypeStruct((L, B, D_S5), F32),
        scratch_shapes=[
            pltpu.VMEM((NG, S5_BATCH, n_state), F32),
            pltpu.VMEM((NG, T * S5_BATCH, n_state), F32),
            pltpu.VMEM((NG, T * S5_BATCH, n_state), BF16),
        ],
        compiler_params=pltpu.CompilerParams(
            dimension_semantics=("parallel", "arbitrary"), vmem_limit_bytes=VMEM_LIMIT),
        name="s5",
    )(u_tm, a_re, a_im, b4, c2, d_skip, w_glu, b_glu)


def _rwkv_kernel(q_ref, mu_ref, w0_ref, wwa_ref, a0_ref, gup_ref, kk_ref, ka_ref, rk_ref,
                 gnw_ref, gnb_ref, o_ref, s_scr, prev_scr, lhs_scr, rhs_scr, v_scr, bk_scr, we_scr,
                 bonus_scr, g_scr, *, C, Bt):
    @pl.when(pl.program_id(1) == 0)
    def _():
        for ref in (s_scr, prev_scr, lhs_scr, rhs_scr, v_scr, bk_scr, we_scr, bonus_scr, g_scr):
            ref[...] = jnp.zeros_like(ref)

    row = lax.broadcasted_iota(jnp.int32, (C, LANES), 0)
    lane = lax.broadcasted_iota(jnp.int32, (C, LANES), 1)
    col = lane & (RWKV_HEAD - 1)
    head0 = lane < RWKV_HEAD
    strict = col < row
    incl = col <= row
    eye = (col == row).astype(F32)
    blk = row ^ col
    r128 = lax.broadcasted_iota(jnp.int32, (LANES, LANES), 0)
    c128 = lax.broadcasted_iota(jnp.int32, (LANES, LANES), 1)
    same_head = (r128 < RWKV_HEAD) == (c128 < RWKV_HEAD)
    ones_blk = same_head.astype(BF16)
    tril = (lax.broadcasted_iota(jnp.int32, (C, C), 0)
            >= lax.broadcasted_iota(jnp.int32, (C, C), 1)).astype(BF16)
    rows = Bt * C
    sl = [slice(p * LANES, (p + 1) * LANES) for p in range(RW_PAIRS)]
    bis = list(range(Bt))
    ch = [(bi, p) for bi in bis for p in range(RW_PAIRS)]
    n_ch = len(ch)

    def bd(y):
        zero = jnp.zeros_like(y)
        return jnp.concatenate([jnp.where(head0, y, zero), jnp.where(head0, zero, y)], axis=0)

    def bmm(x, y):
        return _dot(x, bd(y))

    def headsum(x):
        n = x.shape[0]
        tall = jnp.concatenate([x[:, s_] for s_ in sl], axis=0)
        t = jnp.dot(tall.astype(BF16), ones_blk, preferred_element_type=F32)
        return jnp.concatenate([t[p * n:(p + 1) * n] for p in range(RW_PAIRS)], axis=-1)

    nt = (((1,), (1,)), ((), ()))
    tn = (((0,), (0,)), ((), ()))

    def each(fn, *lists):
        return [fn(*args) for args in zip(*lists)]

    lhs = [lhs_scr[i] for i in range(n_ch)]
    rhs = [rhs_scr[i] for i in range(n_ch)]
    vp = [v_scr[i] for i in range(n_ch)]
    bk = [bk_scr[i] for i in range(n_ch)]
    wep = [we_scr[bi][:, sl[p]] for bi, p in ch]
    S = [s_scr[bi, p] for bi, p in ch]
    bonus_prev = bonus_scr[...]
    g_prev = g_scr[...]

    zero = jnp.zeros((C, LANES), BF16)
    lhs_m = each(lambda t: jnp.concatenate(
        [jnp.where(head0, t[:C], zero), jnp.where(head0, zero, t[:C]),
         jnp.where(head0, t[C:], zero), jnp.where(head0, zero, t[C:])], axis=0), lhs)
    gram = each(lambda x, y: lax.dot_general(x, y, nt, preferred_element_type=F32), lhs_m, rhs)
    swap = lambda t: pltpu.roll(t, RWKV_HEAD, 1)
    a_ab = [jnp.where(strict, jnp.where(head0, t[:C], swap(t[C:2 * C])), 0.0) for t in gram]
    a_ak = [jnp.where(strict, jnp.where(head0, swap(t[:C]), t[C:2 * C]), 0.0) for t in gram]
    r_b = [jnp.where(incl, jnp.where(head0, t[2 * C:3 * C], swap(t[3 * C:])), 0.0) for t in gram]
    r_k = [jnp.where(incl, jnp.where(head0, swap(t[2 * C:3 * C]), t[3 * C:]), 0.0) for t in gram]
    ss = each(lambda x, st: lax.dot_general(x, st.astype(BF16), nt, preferred_element_type=F32),
              lhs, S)
    akv = each(lambda aa, rk_, vv: bmm(jnp.concatenate([aa, rk_], axis=0), vv), a_ak, r_k, vp)
    pm = [jnp.where(blk < 8, t, 0.0) for t in a_ab]
    p2 = each(bmm, pm, pm)

    q = q_ref[...].reshape(rows, N_RW_COLS)
    rolled = pltpu.roll(q, 1, 0)
    first = lax.broadcasted_iota(jnp.int32, (SUBLANES, N_RW_COLS), 0) == 0
    pieces = []
    for bi in bis:
        pieces.append(jnp.where(first, prev_scr[bi], rolled[bi * C:bi * C + SUBLANES]))
        pieces.append(rolled[bi * C + SUBLANES:(bi + 1) * C])
        prev_scr[bi] = q[(bi + 1) * C - 1:(bi + 1) * C, :]
    prev = jnp.concatenate(pieces, axis=0)
    qs = q + mu_ref[...] * (prev - q)
    r = qs[:, 0:D_RWKV]
    k = qs[:, D_RWKV:2 * D_RWKV]
    v = qs[:, 2 * D_RWKV:3 * D_RWKV]
    xwa = qs[:, 3 * D_RWKV:3 * D_RWKV + LANES]
    head0r = lax.broadcasted_iota(jnp.int32, (rows, LANES), 1) < RWKV_HEAD
    wa = _dot(jnp.where(head0r, jnp.tanh(xwa), xwa), wwa_ref[...])
    g = _dot(jax.nn.sigmoid(qs[:, 3 * D_RWKV + LANES:]), gup_ref[...])
    kk = k * kk_ref[...]
    kss = headsum(kk * kk)

    ip = [eye + t for t in pm]
    both = each(lambda x, y: bmm(jnp.concatenate([x, y], axis=0), x), p2, ip)
    t_inv = each(lambda x, y: x + y[C:], ip, both)
    t_inv = each(lambda x, y: x + bmm(x, y[:C]), t_inv, both)

    def double(t_in, gsz):
        m = [jnp.where((blk >= gsz) & (blk < 2 * gsz), t, 0.0) for t in a_ab]
        tm = each(bmm, t_in, m)
        return each(lambda x, y: x + bmm(y, x), t_in, tm)

    t_inv = double(t_inv, 8)

    lw = -math.exp(-0.5) * jax.nn.sigmoid(w0_ref[...] + wa[:, :D_RWKV])
    a = jax.nn.sigmoid(a0_ref[...] + wa[:, D_RWKV:])
    kk = kk / jnp.maximum(jnp.sqrt(kss), 1e-12)
    k2 = k * (1.0 + (a - 1.0) * ka_ref[...])
    bonus = headsum(r * k2 * rk_ref[...]) * v
    lw_hi = lw.astype(BF16)
    lw_lo = (lw - lw_hi.astype(F32)).astype(BF16)
    cum = jnp.concatenate(
        [jnp.dot(tril, lw_hi[bi * C:(bi + 1) * C], preferred_element_type=F32)
         + jnp.dot(tril, lw_lo[bi * C:(bi + 1) * C], preferred_element_type=F32) for bi in bis],
        axis=0)

    gsz = 16
    while gsz < C:
        t_inv = double(t_inv, gsz)
        gsz *= 2
    x = each(lambda s_, t: s_[:C] + t[:C], ss, akv)
    u = each(bmm, t_inv, x)
    o = each(lambda s_, t, rb, uu: s_[C:] + t[C:] + bmm(rb, uu), ss, akv, r_b, u)
    upd = each(lambda uu, vv, bb: lax.dot_general(
        jnp.concatenate([uu.astype(BF16), vv], axis=0), bb, tn, preferred_element_type=F32),
        u, vp, bk)
    for (bi, p), st, we, up in zip(ch, S, wep, upd):
        s_scr[bi, p] = st * we + jnp.where(same_head, up, 0.0)
    o = jnp.concatenate(
        [jnp.concatenate(o[bi * RW_PAIRS:(bi + 1) * RW_PAIRS], axis=-1) for bi in bis], axis=0)
    dlt = o - headsum(o) * (1.0 / RWKV_HEAD)
    var = headsum(dlt * dlt) * (1.0 / RWKV_HEAD)
    on = dlt * lax.rsqrt(var + GN_EPS) * gnw_ref[...] + gnb_ref[...]
    o_ref[...] = ((on + bonus_prev) * g_prev).reshape(Bt, C, D_RWKV)

    w_in = jnp.exp(cum)
    w_inv = jnp.exp(-cum)
    ah = -kk * jnp.exp(cum - lw)
    bh = kk * a * w_inv
    kh = k2 * w_inv
    rh = r * w_in

    for i, (bi, p) in enumerate(ch):
        rs = slice(bi * C, (bi + 1) * C)
        we = w_in[(bi + 1) * C - 1:(bi + 1) * C, sl[p]]
        lhs_scr[i] = jnp.concatenate([ah[rs, sl[p]], rh[rs, sl[p]]], axis=0).astype(BF16)
        rhs_scr[i] = jnp.concatenate([bh[rs, sl[p]], kh[rs, sl[p]]], axis=0).astype(BF16)
        bk_scr[i] = jnp.concatenate([bh[rs, sl[p]] * we, kh[rs, sl[p]] * we], axis=0).astype(BF16)
        v_scr[i] = v[rs, sl[p]].astype(BF16)
    for bi in bis:
        we_scr[bi] = w_in[(bi + 1) * C - 1:(bi + 1) * C, :]
    bonus_scr[...] = bonus
    g_scr[...] = g


def _rwkv(q, mu, w0, wwa, a0, g_up, k_k, k_a, r_k, gn_w, gn_b, *, Bt=RW_BT):
    B, L, _ = q.shape
    C = RW_CHUNK
    n_chunks = L // C
    n_ch = Bt * RW_PAIRS
    const = lambda *shape: pl.BlockSpec(shape, lambda b, c: (0,) * len(shape))
    return pl.pallas_call(
        functools.partial(_rwkv_kernel, C=C, Bt=Bt),
        grid=(B // Bt, n_chunks + 1),
        in_specs=[
            pl.BlockSpec((Bt, C, N_RW_COLS), lambda b, c: (b, jnp.minimum(c, n_chunks - 1), 0)),
            const(1, N_RW_COLS),
            const(1, D_RWKV),
            const(LANES, 2 * D_RWKV),
            const(1, D_RWKV),
            const(G_RANK, D_RWKV),
            const(1, D_RWKV),
            const(1, D_RWKV),
            const(1, D_RWKV),
            const(1, D_RWKV),
            const(1, D_RWKV),
        ],
        out_specs=pl.BlockSpec((Bt, C, D_RWKV), lambda b, c: (b, jnp.maximum(c - 1, 0), 0)),
        out_shape=jax.ShapeDtypeStruct((B, L, D_RWKV), F32),
        scratch_shapes=[
            pltpu.VMEM((Bt, RW_PAIRS, LANES, LANES), F32),
            pltpu.VMEM((Bt, 1, N_RW_COLS), F32),
            pltpu.VMEM((n_ch, 2 * C, LANES), BF16),
            pltpu.VMEM((n_ch, 2 * C, LANES), BF16),
            pltpu.VMEM((n_ch, C, LANES), BF16),
            pltpu.VMEM((n_ch, 2 * C, LANES), BF16),
            pltpu.VMEM((Bt, 1, D_RWKV), F32),
            pltpu.VMEM((Bt * C, D_RWKV), F32),
            pltpu.VMEM((Bt * C, D_RWKV), F32),
        ],
        compiler_params=pltpu.CompilerParams(
            dimension_semantics=("parallel", "arbitrary"), vmem_limit_bytes=VMEM_LIMIT),
        name="rwkv7",
    )(q, mu, w0, wwa, a0, g_up, k_k, k_a, r_k, gn_w, gn_b)


def _ffn_kernel(ys_ref, yr_ref, x_ref, wo_ref, gpm_ref, gpre_ref, wup_ref, wdn_ref, gpost_ref,
                o_ref):
    mix = _dot(ys_ref[...], wo_ref[:D_S5, :]) + _dot(yr_ref[0], wo_ref[D_S5:, :])
    h = x_ref[0] + _rms(mix, gpm_ref[...])
    hn = _rms(h, gpre_ref[...])
    up = _dot(hn, wup_ref[...])
    ff = _dot(jnp.square(jnp.maximum(up, 0.0)), wdn_ref[...])
    o_ref[0] = h + _rms(ff, gpost_ref[...])


def _ffn(ys_tm, yr, x, w_out, g_post_mix, g_pre_mlp, w_up, w_down, g_post_mlp):
    B, L, _ = x.shape
    tl = min(ROW_TILE, L)
    const = lambda *shape: pl.BlockSpec(shape, lambda b, i: (0,) * len(shape),
                                        pipeline_mode=pl.Buffered(1))
    return pl.pallas_call(
        _ffn_kernel,
        grid=(B, L // tl),
        in_specs=[
            pl.BlockSpec((tl, D_S5), lambda b, i: (i, b)),
            pl.BlockSpec((1, tl, D_RWKV), lambda b, i: (b, i, 0)),
            pl.BlockSpec((1, tl, D_MODEL), lambda b, i: (b, i, 0)),
            const(D_MODEL, D_MODEL),
            const(1, D_MODEL),
            const(1, D_MODEL),
            const(D_MODEL, D_FF),
            const(D_FF, D_MODEL),
            const(1, D_MODEL),
        ],
        out_specs=pl.BlockSpec((1, tl, D_MODEL), lambda b, i: (b, i, 0)),
        out_shape=jax.ShapeDtypeStruct((B, L, D_MODEL), F32),
        compiler_params=pltpu.CompilerParams(
            dimension_semantics=("parallel", "parallel"), vmem_limit_bytes=VMEM_LIMIT),
        name="ffn",
    )(ys_tm.reshape(L, B * D_S5), yr, x, w_out, g_post_mix, g_pre_mlp, w_up, w_down, g_post_mlp)


def _layer(x, g_pre_mix, w_in, s5_lam_re, s5_lam_im, s5_log_dt, s5_b_re, s5_b_im, s5_c_re, s5_c_im,
           s5_d, s5_w_glu, s5_b_glu, rw_mu, rw_w0, rw_w_up, rw_a0, rw_a_up, rw_g_up, rw_k_k, rw_k_a,
           rw_r_k, rw_gn_w, rw_gn_b, w_out, g_post_mix, g_pre_mlp, w_ff_up, w_ff_down, g_post_mlp):
    row = lambda t: t.reshape(1, -1)
    u_tm, q = _inproj(x, row(g_pre_mix), w_in.astype(BF16))
    a_re, a_im, b4, c2 = _s5_params(s5_lam_re, s5_lam_im, s5_log_dt, s5_b_re, s5_b_im,
                                    s5_c_re, s5_c_im)
    ys_tm = _s5(u_tm, a_re, a_im, b4, c2, row(s5_d), s5_w_glu.astype(BF16), row(s5_b_glu))
    zeros = jnp.zeros((W_RANK, D_RWKV), F32)
    wwa = jnp.concatenate([jnp.concatenate([rw_w_up, zeros], axis=1),
                           jnp.concatenate([zeros, rw_a_up], axis=1)], axis=0).astype(BF16)
    yr = _rwkv(q, row(rw_mu), row(rw_w0), wwa, row(rw_a0), rw_g_up.astype(BF16), row(rw_k_k),
               row(rw_k_a), row(rw_r_k), row(rw_gn_w), row(rw_gn_b))
    return _ffn(ys_tm, yr, x, w_out.astype(BF16), row(g_post_mix), row(g_pre_mlp),
                w_ff_up.astype(BF16), w_ff_down.astype(BF16), row(g_post_mlp))


def kernel(x, g_pre_mix, w_in, s5_lam_re, s5_lam_im, s5_log_dt, s5_b_re, s5_b_im, s5_c_re, s5_c_im,
           s5_d, s5_w_glu, s5_b_glu, rw_mu, rw_w0, rw_w_up, rw_a0, rw_a_up, rw_g_up, rw_k_k, rw_k_a,
           rw_r_k, rw_gn_w, rw_gn_b, w_out, g_post_mix, g_pre_mlp, w_ff_up, w_ff_down, g_post_mlp):
    params = (g_pre_mix, w_in, s5_lam_re, s5_lam_im, s5_log_dt, s5_b_re, s5_b_im, s5_c_re, s5_c_im,
              s5_d, s5_w_glu, s5_b_glu, rw_mu, rw_w0, rw_w_up, rw_a0, rw_a_up, rw_g_up, rw_k_k,
              rw_k_a, rw_r_k, rw_gn_w, rw_gn_b, w_out, g_post_mix, g_pre_mlp, w_ff_up, w_ff_down,
              g_post_mlp)
    h = x
    for layer in range(g_pre_mix.shape[0]):
        h = _layer(h, *[p[layer] for p in params])
    return h
```

```python
import functools
import math

import jax
import jax.numpy as jnp
from jax import lax
from jax.experimental import pallas as pl
from jax.experimental.pallas import tpu as pltpu

F32 = jnp.float32
BF16 = jnp.bfloat16

D_MODEL = 1024
D_S5 = 512
D_RWKV = 512
S5_GROUP = 16
S5_GROUPS = 32
S5_STATE = 64
RWKV_HEAD = 64
RWKV_HEADS = 8
W_RANK = 64
A_RANK = 64
G_RANK = 128
N_RW_COLS = 3 * D_RWKV + W_RANK + A_RANK + G_RANK
N_PROJ = D_S5 + N_RW_COLS
D_FF = 4 * D_MODEL
RMS_EPS = 1e-6
GN_EPS = 64e-5

SUBLANES = 8
LANES = 128

ROW_TILE = 512
S5_BATCH = SUBLANES
S5_T = 64
S5_GROUPS_PER_STEP = 2
S5_BLOCK_GROUPS = LANES // S5_GROUP
S5_BLOCKS = S5_GROUPS // S5_BLOCK_GROUPS
S5_BLOCK_STATE = S5_BLOCK_GROUPS * S5_STATE
S5_BLOCK_COLS = 2 * S5_BLOCK_STATE
RW_CHUNK = 64
RW_PAIRS = RWKV_HEADS // 2
RW_BT = 4
VMEM_LIMIT = 56 * 1024 * 1024


def _rms(x, g):
    return x * lax.rsqrt(jnp.mean(x * x, axis=-1, keepdims=True) + RMS_EPS) * g


def _dot(a, b):
    return jnp.dot(a.astype(BF16), b.astype(BF16), preferred_element_type=F32)


def _inproj_kernel(x_ref, g_ref, w_ref, mu_ref, u_ref, q_ref, prev_scr, *, nb, T):
    @pl.when(pl.program_id(1) == 0)
    def _():
        prev_scr[...] = jnp.zeros_like(prev_scr)

    xn = _rms(x_ref[...].reshape(nb * T, D_MODEL), g_ref[...])
    p = _dot(xn, w_ref[...])
    for b in range(nb):
        u_ref[:, b, :] = p[b * T:(b + 1) * T, :D_S5]
    q = p[:, D_S5:]
    rolled = pltpu.roll(q, 1, 0)
    first = lax.broadcasted_iota(jnp.int32, (SUBLANES, N_RW_COLS), 0) == 0
    pieces = []
    for b in range(nb):
        pieces.append(jnp.where(first, prev_scr[b], rolled[b * T:b * T + SUBLANES]))
        pieces.append(rolled[b * T + SUBLANES:(b + 1) * T])
        prev_scr[b] = q[(b + 1) * T - 1:(b + 1) * T, :]
    prev = jnp.concatenate(pieces, axis=0)
    q_ref[...] = (q + mu_ref[...] * (prev - q)).reshape(nb, T, N_RW_COLS)


def _inproj(x, g, w_in, mu):
    B, L, _ = x.shape
    nb = SUBLANES
    T = min(ROW_TILE // nb, L)
    return pl.pallas_call(
        functools.partial(_inproj_kernel, nb=nb, T=T),
        grid=(B // nb, L // T),
        in_specs=[
            pl.BlockSpec((nb, T, D_MODEL), lambda b, i: (b, i, 0)),
            pl.BlockSpec((1, D_MODEL), lambda b, i: (0, 0)),
            pl.BlockSpec((D_MODEL, N_PROJ), lambda b, i: (0, 0)),
            pl.BlockSpec((1, N_RW_COLS), lambda b, i: (0, 0)),
        ],
        out_specs=[
            pl.BlockSpec((T, nb, D_S5), lambda b, i: (i, b, 0)),
            pl.BlockSpec((nb, T, N_RW_COLS), lambda b, i: (b, i, 0)),
        ],
        out_shape=[
            jax.ShapeDtypeStruct((L, B, D_S5), F32),
            jax.ShapeDtypeStruct((B, L, N_RW_COLS), F32),
        ],
        scratch_shapes=[pltpu.VMEM((nb, 1, N_RW_COLS), F32)],
        compiler_params=pltpu.CompilerParams(
            dimension_semantics=("parallel", "arbitrary"), vmem_limit_bytes=VMEM_LIMIT),
        name="inproj",
    )(x, g, w_in, mu)


def _s5_kernel(u_ref, are_ref, aim_ref, b4_ref, c2_ref, d_ref, wglu_ref, bglu_ref,
               y_ref, s_scr, bu_scr, sb_scr, *, T, NG):
    rows = T * S5_BATCH
    pair_cols = 2 * S5_BLOCK_COLS

    @pl.when(pl.program_id(1) == 0)
    def _():
        s_scr[...] = jnp.zeros_like(s_scr)

    us = [u_ref[:, g * S5_BATCH:(g + 1) * S5_BATCH, :].reshape(rows, D_S5) for g in range(NG)]
    ubs = [u.astype(BF16) for u in us]

    def bu(g, h):
        for q in (2 * h, 2 * h + 1):
            bu_scr[g, :, q * S5_BLOCK_COLS:(q + 1) * S5_BLOCK_COLS] = jnp.dot(
                ubs[g][:, q * LANES:(q + 1) * LANES], b4_ref[q], preferred_element_type=F32)

    def scan(g, h):
        qs = (2 * h, 2 * h + 1)
        a_bc = {}
        st = {}
        for q in qs:
            a0 = q * S5_BLOCK_STATE
            a_bc[q] = (
                jnp.broadcast_to(are_ref[:, a0:a0 + S5_BLOCK_STATE], (S5_BATCH, S5_BLOCK_STATE)),
                jnp.broadcast_to(aim_ref[:, a0:a0 + S5_BLOCK_STATE], (S5_BATCH, S5_BLOCK_STATE)))
            re0 = q * S5_BLOCK_COLS
            st[q] = (s_scr[g, :, re0:re0 + S5_BLOCK_STATE],
                     s_scr[g, :, re0 + S5_BLOCK_STATE:re0 + S5_BLOCK_COLS])
        held = {}
        for t in range(T):
            r0 = t * S5_BATCH
            for q in qs:
                re0 = q * S5_BLOCK_COLS
                im0 = re0 + S5_BLOCK_STATE
                are, aim = a_bc[q]
                sre, sim = st[q]
                nre = are * sre - aim * sim + bu_scr[g, r0:r0 + S5_BATCH, re0:re0 + S5_BLOCK_STATE]
                nim = are * sim + aim * sre + bu_scr[g, r0:r0 + S5_BATCH, im0:im0 + S5_BLOCK_STATE]
                st[q] = (nre, nim)
                if t % 2 == 0:
                    held[q] = (nre, nim)
                else:
                    p0 = r0 - S5_BATCH
                    sb_scr[g, p0:p0 + 2 * S5_BATCH, re0:re0 + S5_BLOCK_STATE] = jnp.concatenate(
                        [held[q][0], nre], axis=0).astype(BF16)
                    sb_scr[g, p0:p0 + 2 * S5_BATCH, im0:im0 + S5_BLOCK_STATE] = jnp.concatenate(
                        [held[q][1], nim], axis=0).astype(BF16)
        for q in qs:
            re0 = q * S5_BLOCK_COLS
            s_scr[g, :, re0:re0 + S5_BLOCK_STATE] = st[q][0]
            s_scr[g, :, re0 + S5_BLOCK_STATE:re0 + S5_BLOCK_COLS] = st[q][1]

    def cs(g, h):
        return jnp.dot(sb_scr[g, :, h * pair_cols:(h + 1) * pair_cols], c2_ref[h],
                       preferred_element_type=F32)

    def gelu_in(g, ys):
        return jax.nn.gelu(jnp.concatenate(ys, axis=-1) + d_ref[...] * us[g])

    def glu_out(g, z):
        gate = jax.nn.sigmoid(_dot(z, wglu_ref[...]) + bglu_ref[...])
        y_ref[:, g * S5_BATCH:(g + 1) * S5_BATCH, :] = (z * gate).reshape(T, S5_BATCH, D_S5)

    n_pairs = S5_BLOCKS // 2
    for h in range(n_pairs):
        bu(0, h)
    ys = {g: [] for g in range(NG)}
    for g in range(NG):
        for h in range(n_pairs):
            scan(g, h)
            if g + 1 < NG:
                bu(g + 1, h)
            ys[g].append(cs(g, h))
    zs = [gelu_in(g, ys[g]) for g in range(NG)]
    for g in range(NG):
        glu_out(g, zs[g])


def _s5_params(lam_re, lam_im, log_dt, b_re, b_im, c_re, c_im):
    dt = jnp.exp(log_dt)[:, None]
    mag = jnp.exp(lam_re * dt)
    ab_re = mag * jnp.cos(lam_im * dt)
    ab_im = mag * jnp.sin(lam_im * dt)
    den = lam_re * lam_re + lam_im * lam_im
    nr = ab_re - 1.0
    f_re = (nr * lam_re + ab_im * lam_im) / den
    f_im = (ab_im * lam_re - nr * lam_im) / den
    bb_re = f_re[..., None] * b_re - f_im[..., None] * b_im
    bb_im = f_re[..., None] * b_im + f_im[..., None] * b_re
    eye = jnp.eye(S5_BLOCK_GROUPS, dtype=F32)

    def blk(t, q):
        return t[q * S5_BLOCK_GROUPS:(q + 1) * S5_BLOCK_GROUPS]

    def b_block(bb, q):
        return jnp.einsum("gpc,gk->gckp", blk(bb, q), eye).reshape(LANES, S5_BLOCK_STATE)

    def c_block(cc, q):
        return jnp.einsum("gcp,gk->gpkc", blk(cc, q), eye).reshape(S5_BLOCK_STATE, LANES)

    b4 = jnp.stack([jnp.concatenate([b_block(bb_re, q), b_block(bb_im, q)], axis=1)
                    for q in range(S5_BLOCKS)]).astype(BF16)
    c4 = [jnp.concatenate([c_block(c_re, q), -c_block(c_im, q)], axis=0)
          for q in range(S5_BLOCKS)]
    zc = jnp.zeros_like(c4[0])
    c2 = jnp.stack([jnp.concatenate([jnp.concatenate([c4[2 * h], zc], axis=1),
                                     jnp.concatenate([zc, c4[2 * h + 1]], axis=1)], axis=0)
                    for h in range(S5_BLOCKS // 2)]).astype(BF16)
    return ab_re.reshape(1, -1), ab_im.reshape(1, -1), b4, c2


def _s5(u_tm, a_re, a_im, b4, c2, d_skip, w_glu, b_glu):
    L, B, _ = u_tm.shape
    T = min(S5_T, L)
    NG = min(S5_GROUPS_PER_STEP, B // S5_BATCH)
    n_state = S5_BLOCKS * S5_BLOCK_COLS
    const = lambda *shape: pl.BlockSpec(shape, lambda b, t: (0,) * len(shape))
    return pl.pallas_call(
        functools.partial(_s5_kernel, T=T, NG=NG),
        grid=(B // (NG * S5_BATCH), L // T),
        in_specs=[
            pl.BlockSpec((T, NG * S5_BATCH, D_S5), lambda b, t: (t, b, 0)),
            const(1, S5_GROUPS * S5_STATE),
            const(1, S5_GROUPS * S5_STATE),
            const(S5_BLOCKS, LANES, S5_BLOCK_COLS),
            const(S5_BLOCKS // 2, 2 * S5_BLOCK_COLS, 2 * LANES),
            const(1, D_S5),
            const(D_S5, D_S5),
            const(1, D_S5),
        ],
        out_specs=pl.BlockSpec((T, NG * S5_BATCH, D_S5), lambda b, t: (t, b, 0)),
        out_shape=jax.ShapeDtypeStruct((L, B, D_S5), F32),
        scratch_shapes=[
            pltpu.VMEM((NG, S5_BATCH, n_state), F32),
            pltpu.VMEM((NG, T * S5_BATCH, n_state), F32),
            pltpu.VMEM((NG, T * S5_BATCH, n_state), BF16),
        ],
        compiler_params=pltpu.CompilerParams(
            dimension_semantics=("parallel", "arbitrary"), vmem_limit_bytes=VMEM_LIMIT),
        name="s5",
    )(u_tm, a_re, a_im, b4, c2, d_skip, w_glu, b_glu)


def _rwkv_kernel(q_ref, w0_ref, wwa_ref, a0_ref, gup_ref, kk_ref, ka_ref, rk_ref,
                 gnw_ref, gnb_ref, o_ref, s_scr, lhs_scr, rhs_scr, v_scr, bk_scr, we_scr,
                 bonus_scr, g_scr, *, C, Bt):
    @pl.when(pl.program_id(1) == 0)
    def _():
        for ref in (s_scr, lhs_scr, rhs_scr, v_scr, bk_scr, we_scr, bonus_scr, g_scr):
            ref[...] = jnp.zeros_like(ref)

    row = lax.broadcasted_iota(jnp.int32, (C, LANES), 0)
    lane = lax.broadcasted_iota(jnp.int32, (C, LANES), 1)
    col = lane & (RWKV_HEAD - 1)
    head0 = lane < RWKV_HEAD
    strict = col < row
    incl = col <= row
    eye = (col == row).astype(F32)
    blk = row ^ col
    r128 = lax.broadcasted_iota(jnp.int32, (LANES, LANES), 0)
    c128 = lax.broadcasted_iota(jnp.int32, (LANES, LANES), 1)
    same_head = (r128 < RWKV_HEAD) == (c128 < RWKV_HEAD)
    ones_blk = same_head.astype(BF16)
    tril = (lax.broadcasted_iota(jnp.int32, (C, C), 0)
            >= lax.broadcasted_iota(jnp.int32, (C, C), 1)).astype(BF16)
    rows = Bt * C
    sl = [slice(p * LANES, (p + 1) * LANES) for p in range(RW_PAIRS)]
    bis = list(range(Bt))
    ch = [(bi, p) for bi in bis for p in range(RW_PAIRS)]
    n_ch = len(ch)

    def bd(y):
        zero = jnp.zeros_like(y)
        return jnp.concatenate([jnp.where(head0, y, zero), jnp.where(head0, zero, y)], axis=0)

    def bmm(x, y):
        return _dot(x, bd(y))

    def headsum(x):
        n = x.shape[0]
        tall = jnp.concatenate([x[:, s_] for s_ in sl], axis=0)
        t = jnp.dot(tall.astype(BF16), ones_blk, preferred_element_type=F32)
        return jnp.concatenate([t[p * n:(p + 1) * n] for p in range(RW_PAIRS)], axis=-1)

    nt = (((1,), (1,)), ((), ()))
    tn = (((0,), (0,)), ((), ()))

    def each(fn, *lists):
        return [fn(*args) for args in zip(*lists)]

    lhs = [lhs_scr[i] for i in range(n_ch)]
    rhs = [rhs_scr[i] for i in range(n_ch)]
    vp = [v_scr[i] for i in range(n_ch)]
    bk = [bk_scr[i] for i in range(n_ch)]
    wep = [we_scr[bi][:, sl[p]] for bi, p in ch]
    S = [s_scr[bi, p] for bi, p in ch]
    bonus_prev = bonus_scr[...]
    g_prev = g_scr[...]

    zero = jnp.zeros((C, LANES), BF16)
    lhs_m = each(lambda t: jnp.concatenate(
        [jnp.where(head0, t[:C], zero), jnp.where(head0, zero, t[:C]),
         jnp.where(head0, t[C:], zero), jnp.where(head0, zero, t[C:])], axis=0), lhs)
    gram = each(lambda x, y: lax.dot_general(x, y, nt, preferred_element_type=F32), lhs_m, rhs)
    swap = lambda t: pltpu.roll(t, RWKV_HEAD, 1)
    a_ab = [jnp.where(strict, jnp.where(head0, t[:C], swap(t[C:2 * C])), 0.0) for t in gram]
    a_ak = [jnp.where(strict, jnp.where(head0, swap(t[:C]), t[C:2 * C]), 0.0) for t in gram]
    r_b = [jnp.where(incl, jnp.where(head0, t[2 * C:3 * C], swap(t[3 * C:])), 0.0) for t in gram]
    r_k = [jnp.where(incl, jnp.where(head0, swap(t[2 * C:3 * C]), t[3 * C:]), 0.0) for t in gram]
    ss = each(lambda x, st: lax.dot_general(x, st.astype(BF16), nt, preferred_element_type=F32),
              lhs, S)
    akv = each(lambda aa, rk_, vv: bmm(jnp.concatenate([aa, rk_], axis=0), vv), a_ak, r_k, vp)
    pm = [jnp.where(blk < 8, t, 0.0) for t in a_ab]
    p2 = each(bmm, pm, pm)

    qs = q_ref[...].reshape(rows, N_RW_COLS)
    r = qs[:, 0:D_RWKV]
    k = qs[:, D_RWKV:2 * D_RWKV]
    v = qs[:, 2 * D_RWKV:3 * D_RWKV]
    xwa = qs[:, 3 * D_RWKV:3 * D_RWKV + LANES]
    head0r = lax.broadcasted_iota(jnp.int32, (rows, LANES), 1) < RWKV_HEAD
    wa = _dot(jnp.where(head0r, jnp.tanh(xwa), xwa), wwa_ref[...])
    g = _dot(jax.nn.sigmoid(qs[:, 3 * D_RWKV + LANES:]), gup_ref[...])
    kk = k * kk_ref[...]
    kss = headsum(kk * kk)

    ip = [eye + t for t in pm]
    both = each(lambda x, y: bmm(jnp.concatenate([x, y], axis=0), x), p2, ip)
    t_inv = each(lambda x, y: x + y[C:], ip, both)
    t_inv = each(lambda x, y: x + bmm(x, y[:C]), t_inv, both)

    def double(t_in, gsz):
        m = [jnp.where((blk >= gsz) & (blk < 2 * gsz), t, 0.0) for t in a_ab]
        tm = each(bmm, t_in, m)
        return each(lambda x, y: x + bmm(y, x), t_in, tm)

    t_inv = double(t_inv, 8)

    lw = -math.exp(-0.5) * jax.nn.sigmoid(w0_ref[...] + wa[:, :D_RWKV])
    a = jax.nn.sigmoid(a0_ref[...] + wa[:, D_RWKV:])
    kk = kk / jnp.maximum(jnp.sqrt(kss), 1e-12)
    k2 = k * (1.0 + (a - 1.0) * ka_ref[...])
    bonus = headsum(r * k2 * rk_ref[...]) * v
    lw_hi = lw.astype(BF16)
    lw_lo = (lw - lw_hi.astype(F32)).astype(BF16)
    cum = jnp.concatenate(
        [jnp.dot(tril, lw_hi[bi * C:(bi + 1) * C], preferred_element_type=F32)
         + jnp.dot(tril, lw_lo[bi * C:(bi + 1) * C], preferred_element_type=F32) for bi in bis],
        axis=0)

    gsz = 16
    while gsz < C:
        t_inv = double(t_inv, gsz)
        gsz *= 2
    x = each(lambda s_, t: s_[:C] + t[:C], ss, akv)
    u = each(bmm, t_inv, x)
    o = each(lambda s_, t, rb, uu: s_[C:] + t[C:] + bmm(rb, uu), ss, akv, r_b, u)
    upd = each(lambda uu, vv, bb: lax.dot_general(
        jnp.concatenate([uu.astype(BF16), vv], axis=0), bb, tn, preferred_element_type=F32),
        u, vp, bk)
    for (bi, p), st, we, up in zip(ch, S, wep, upd):
        s_scr[bi, p] = st * we + jnp.where(same_head, up, 0.0)
    o = jnp.concatenate(
        [jnp.concatenate(o[bi * RW_PAIRS:(bi + 1) * RW_PAIRS], axis=-1) for bi in bis], axis=0)
    dlt = o - headsum(o) * (1.0 / RWKV_HEAD)
    var = headsum(dlt * dlt) * (1.0 / RWKV_HEAD)
    on = dlt * lax.rsqrt(var + GN_EPS) * gnw_ref[...] + gnb_ref[...]
    o_ref[...] = ((on + bonus_prev) * g_prev).reshape(Bt, C, D_RWKV)

    w_in = jnp.exp(cum)
    w_inv = jnp.exp(-cum)
    ah = -kk * jnp.exp(cum - lw)
    bh = kk * a * w_inv
    kh = k2 * w_inv
    rh = r * w_in

    for i, (bi, p) in enumerate(ch):
        rs = slice(bi * C, (bi + 1) * C)
        we = w_in[(bi + 1) * C - 1:(bi + 1) * C, sl[p]]
        lhs_scr[i] = jnp.concatenate([ah[rs, sl[p]], rh[rs, sl[p]]], axis=0).astype(BF16)
        rhs_scr[i] = jnp.concatenate([bh[rs, sl[p]], kh[rs, sl[p]]], axis=0).astype(BF16)
        bk_scr[i] = jnp.concatenate([bh[rs, sl[p]] * we, kh[rs, sl[p]] * we], axis=0).astype(BF16)
        v_scr[i] = v[rs, sl[p]].astype(BF16)
    for bi in bis:
        we_scr[bi] = w_in[(bi + 1) * C - 1:(bi + 1) * C, :]
    bonus_scr[...] = bonus
    g_scr[...] = g


def _rwkv(q, w0, wwa, a0, g_up, k_k, k_a, r_k, gn_w, gn_b, *, Bt=RW_BT):
    B, L, _ = q.shape
    C = RW_CHUNK
    n_chunks = L // C
    n_ch = Bt * RW_PAIRS
    const = lambda *shape: pl.BlockSpec(shape, lambda b, c: (0,) * len(shape))
    return pl.pallas_call(
        functools.partial(_rwkv_kernel, C=C, Bt=Bt),
        grid=(B // Bt, n_chunks + 1),
        in_specs=[
            pl.BlockSpec((Bt, C, N_RW_COLS), lambda b, c: (b, jnp.minimum(c, n_chunks - 1), 0)),
            const(1, D_RWKV),
            const(LANES, 2 * D_RWKV),
            const(1, D_RWKV),
            const(G_RANK, D_RWKV),
            const(1, D_RWKV),
            const(1, D_RWKV),
            const(1, D_RWKV),
            const(1, D_RWKV),
            const(1, D_RWKV),
        ],
        out_specs=pl.BlockSpec((Bt, C, D_RWKV), lambda b, c: (b, jnp.maximum(c - 1, 0), 0)),
        out_shape=jax.ShapeDtypeStruct((B, L, D_RWKV), F32),
        scratch_shapes=[
            pltpu.VMEM((Bt, RW_PAIRS, LANES, LANES), F32),
            pltpu.VMEM((n_ch, 2 * C, LANES), BF16),
            pltpu.VMEM((n_ch, 2 * C, LANES), BF16),
            pltpu.VMEM((n_ch, C, LANES), BF16),
            pltpu.VMEM((n_ch, 2 * C, LANES), BF16),
            pltpu.VMEM((Bt, 1, D_RWKV), F32),
            pltpu.VMEM((Bt * C, D_RWKV), F32),
            pltpu.VMEM((Bt * C, D_RWKV), F32),
        ],
        compiler_params=pltpu.CompilerParams(
            dimension_semantics=("parallel", "arbitrary"), vmem_limit_bytes=VMEM_LIMIT),
        name="rwkv7",
    )(q, w0, wwa, a0, g_up, k_k, k_a, r_k, gn_w, gn_b)


def _ffn_kernel(ys_ref, yr_ref, x_ref, wo_ref, gpm_ref, gpre_ref, wup_ref, wdn_ref, gpost_ref,
                o_ref):
    mix = _dot(ys_ref[...], wo_ref[:D_S5, :]) + _dot(yr_ref[0], wo_ref[D_S5:, :])
    h = x_ref[0] + _rms(mix, gpm_ref[...])
    hn = _rms(h, gpre_ref[...])
    up = _dot(hn, wup_ref[...])
    ff = _dot(jnp.square(jnp.maximum(up, 0.0)), wdn_ref[...])
    o_ref[0] = h + _rms(ff, gpost_ref[...])


def _ffn(ys_tm, yr, x, w_out, g_post_mix, g_pre_mlp, w_up, w_down, g_post_mlp):
    B, L, _ = x.shape
    tl = min(ROW_TILE, L)
    const = lambda *shape: pl.BlockSpec(shape, lambda b, i: (0,) * len(shape),
                                        pipeline_mode=pl.Buffered(1))
    return pl.pallas_call(
        _ffn_kernel,
        grid=(B, L // tl),
        in_specs=[
            pl.BlockSpec((tl, D_S5), lambda b, i: (i, b)),
            pl.BlockSpec((1, tl, D_RWKV), lambda b, i: (b, i, 0)),
            pl.BlockSpec((1, tl, D_MODEL), lambda b, i: (b, i, 0)),
            const(D_MODEL, D_MODEL),
            const(1, D_MODEL),
            const(1, D_MODEL),
            const(D_MODEL, D_FF),
            const(D_FF, D_MODEL),
            const(1, D_MODEL),
        ],
        out_specs=pl.BlockSpec((1, tl, D_MODEL), lambda b, i: (b, i, 0)),
        out_shape=jax.ShapeDtypeStruct((B, L, D_MODEL), F32),
        compiler_params=pltpu.CompilerParams(
            dimension_semantics=("parallel", "parallel"), vmem_limit_bytes=VMEM_LIMIT),
        name="ffn",
    )(ys_tm.reshape(L, B * D_S5), yr, x, w_out, g_post_mix, g_pre_mlp, w_up, w_down, g_post_mlp)


def _layer(x, g_pre_mix, w_in, s5_lam_re, s5_lam_im, s5_log_dt, s5_b_re, s5_b_im, s5_c_re, s5_c_im,
           s5_d, s5_w_glu, s5_b_glu, rw_mu, rw_w0, rw_w_up, rw_a0, rw_a_up, rw_g_up, rw_k_k, rw_k_a,
           rw_r_k, rw_gn_w, rw_gn_b, w_out, g_post_mix, g_pre_mlp, w_ff_up, w_ff_down, g_post_mlp):
    row = lambda t: t.reshape(1, -1)
    u_tm, q = _inproj(x, row(g_pre_mix), w_in.astype(BF16), row(rw_mu))
    a_re, a_im, b4, c2 = _s5_params(s5_lam_re, s5_lam_im, s5_log_dt, s5_b_re, s5_b_im,
                                    s5_c_re, s5_c_im)
    ys_tm = _s5(u_tm, a_re, a_im, b4, c2, row(s5_d), s5_w_glu.astype(BF16), row(s5_b_glu))
    zeros = jnp.zeros((W_RANK, D_RWKV), F32)
    wwa = jnp.concatenate([jnp.concatenate([rw_w_up, zeros], axis=1),
                           jnp.concatenate([zeros, rw_a_up], axis=1)], axis=0).astype(BF16)
    yr = _rwkv(q, row(rw_w0), wwa, row(rw_a0), rw_g_up.astype(BF16), row(rw_k_k),
               row(rw_k_a), row(rw_r_k), row(rw_gn_w), row(rw_gn_b))
    return _ffn(ys_tm, yr, x, w_out.astype(BF16), row(g_post_mix), row(g_pre_mlp),
                w_ff_up.astype(BF16), w_ff_down.astype(BF16), row(g_post_mlp))


def kernel(x, g_pre_mix, w_in, s5_lam_re, s5_lam_im, s5_log_dt, s5_b_re, s5_b_im, s5_c_re, s5_c_im,
           s5_d, s5_w_glu, s5_b_glu, rw_mu, rw_w0, rw_w_up, rw_a0, rw_a_up, rw_g_up, rw_k_k, rw_k_a,
           rw_r_k, rw_gn_w, rw_gn_b, w_out, g_post_mix, g_pre_mlp, w_ff_up, w_ff_down, g_post_mlp):
    params = (g_pre_mix, w_in, s5_lam_re, s5_lam_im, s5_log_dt, s5_b_re, s5_b_im, s5_c_re, s5_c_im,
              s5_d, s5_w_glu, s5_b_glu, rw_mu, rw_w0, rw_w_up, rw_a0, rw_a_up, rw_g_up, rw_k_k,
              rw_k_a, rw_r_k, rw_gn_w, rw_gn_b, w_out, g_post_mix, g_pre_mlp, w_ff_up, w_ff_down,
              g_post_mlp)
    h = x
    for layer in range(g_pre_mix.shape[0]):
        h = _layer(h, *[p[layer] for p in params])
    return h
```

```python
import functools
import math

import jax
import jax.numpy as jnp
from jax import lax
from jax.experimental import pallas as pl
from jax.experimental.pallas import tpu as pltpu

F32 = jnp.float32
BF16 = jnp.bfloat16

D_MODEL = 1024
D_S5 = 512
D_RWKV = 512
S5_GROUP = 16
S5_GROUPS = 32
S5_STATE = 64
RWKV_HEAD = 64
RWKV_HEADS = 8
W_RANK = 64
A_RANK = 64
G_RANK = 128
N_RW_COLS = 3 * D_RWKV + W_RANK + A_RANK + G_RANK
N_PROJ = D_S5 + N_RW_COLS
D_FF = 4 * D_MODEL
RMS_EPS = 1e-6
GN_EPS = 64e-5

SUBLANES = 8
LANES = 128

ROW_TILE = 512
IN_BATCH = 16
IN_T = 64
S5_BATCH = SUBLANES
S5_T = 64
S5_GROUPS_PER_STEP = 2
S5_BLOCK_GROUPS = LANES // S5_GROUP
S5_BLOCKS = S5_GROUPS // S5_BLOCK_GROUPS
S5_BLOCK_STATE = S5_BLOCK_GROUPS * S5_STATE
S5_BLOCK_COLS = 2 * S5_BLOCK_STATE
RW_CHUNK = 64
RW_PAIRS = RWKV_HEADS // 2
RW_BT = 4
VMEM_LIMIT = 56 * 1024 * 1024


def _rms(x, g):
    return x * lax.rsqrt(jnp.mean(x * x, axis=-1, keepdims=True) + RMS_EPS) * g


def _dot(a, b):
    return jnp.dot(a.astype(BF16), b.astype(BF16), preferred_element_type=F32)


def _inproj_kernel(x_ref, g_ref, w_ref, mu_ref, u_ref, q_ref, prev_scr, *, nb, T):
    @pl.when(pl.program_id(1) == 0)
    def _():
        prev_scr[...] = jnp.zeros_like(prev_scr)

    xn = _rms(x_ref[...].reshape(nb * T, D_MODEL), g_ref[...]).astype(BF16)
    q = jnp.dot(xn, w_ref[:, D_S5:], preferred_element_type=F32)
    u = jnp.dot(xn, w_ref[:, :D_S5], preferred_element_type=F32)
    for b in range(nb):
        u_ref[:, b, :] = u[b * T:(b + 1) * T, :]
    rolled = pltpu.roll(q, 1, 0)
    first = lax.broadcasted_iota(jnp.int32, (SUBLANES, N_RW_COLS), 0) == 0
    pieces = []
    for b in range(nb):
        pieces.append(jnp.where(first, prev_scr[b], rolled[b * T:b * T + SUBLANES]))
        pieces.append(rolled[b * T + SUBLANES:(b + 1) * T])
        prev_scr[b] = q[(b + 1) * T - 1:(b + 1) * T, :]
    prev = jnp.concatenate(pieces, axis=0)
    q_ref[...] = (q + mu_ref[...] * (prev - q)).reshape(nb, T, N_RW_COLS)


def _inproj(x, g, w_in, mu):
    B, L, _ = x.shape
    nb = min(IN_BATCH, B)
    T = min(IN_T, L)
    return pl.pallas_call(
        functools.partial(_inproj_kernel, nb=nb, T=T),
        grid=(B // nb, L // T),
        in_specs=[
            pl.BlockSpec((nb, T, D_MODEL), lambda b, i: (b, i, 0)),
            pl.BlockSpec((1, D_MODEL), lambda b, i: (0, 0)),
            pl.BlockSpec((D_MODEL, N_PROJ), lambda b, i: (0, 0)),
            pl.BlockSpec((1, N_RW_COLS), lambda b, i: (0, 0)),
        ],
        out_specs=[
            pl.BlockSpec((T, nb, D_S5), lambda b, i: (i, b, 0)),
            pl.BlockSpec((nb, T, N_RW_COLS), lambda b, i: (b, i, 0)),
        ],
        out_shape=[
            jax.ShapeDtypeStruct((L, B, D_S5), F32),
            jax.ShapeDtypeStruct((B, L, N_RW_COLS), F32),
        ],
        scratch_shapes=[pltpu.VMEM((nb, 1, N_RW_COLS), F32)],
        compiler_params=pltpu.CompilerParams(
            dimension_semantics=("parallel", "arbitrary"), vmem_limit_bytes=VMEM_LIMIT),
        name="inproj",
    )(x, g, w_in, mu)


def _s5_kernel(u_ref, are_ref, aim_ref, b4_ref, c2_ref, d_ref, wglu_ref, bglu_ref,
               y_ref, s_scr, bu_scr, sb_scr, *, T, NG):
    rows = T * S5_BATCH
    pair_cols = 2 * S5_BLOCK_COLS

    @pl.when(pl.program_id(1) == 0)
    def _():
        s_scr[...] = jnp.zeros_like(s_scr)

    us = [u_ref[:, g * S5_BATCH:(g + 1) * S5_BATCH, :].reshape(rows, D_S5) for g in range(NG)]
    ubs = [u.astype(BF16) for u in us]

    def bu(g, h):
        for q in (2 * h, 2 * h + 1):
            bu_scr[g, :, q * S5_BLOCK_COLS:(q + 1) * S5_BLOCK_COLS] = jnp.dot(
                ubs[g][:, q * LANES:(q + 1) * LANES], b4_ref[q], preferred_element_type=F32)

    def scan(g, h):
        qs = (2 * h, 2 * h + 1)
        a_bc = {}
        st = {}
        for q in qs:
            a0 = q * S5_BLOCK_STATE
            a_bc[q] = (
                jnp.broadcast_to(are_ref[:, a0:a0 + S5_BLOCK_STATE], (S5_BATCH, S5_BLOCK_STATE)),
                jnp.broadcast_to(aim_ref[:, a0:a0 + S5_BLOCK_STATE], (S5_BATCH, S5_BLOCK_STATE)))
            re0 = q * S5_BLOCK_COLS
            st[q] = (s_scr[g, :, re0:re0 + S5_BLOCK_STATE],
                     s_scr[g, :, re0 + S5_BLOCK_STATE:re0 + S5_BLOCK_COLS])
        held = {}
        for t in range(T):
            r0 = t * S5_BATCH
            for q in qs:
                re0 = q * S5_BLOCK_COLS
                im0 = re0 + S5_BLOCK_STATE
                are, aim = a_bc[q]
                sre, sim = st[q]
                nre = are * sre - aim * sim + bu_scr[g, r0:r0 + S5_BATCH, re0:re0 + S5_BLOCK_STATE]
                nim = are * sim + aim * sre + bu_scr[g, r0:r0 + S5_BATCH, im0:im0 + S5_BLOCK_STATE]
                st[q] = (nre, nim)
                if t % 2 == 0:
                    held[q] = (nre, nim)
                else:
                    p0 = r0 - S5_BATCH
                    sb_scr[g, p0:p0 + 2 * S5_BATCH, re0:re0 + S5_BLOCK_STATE] = jnp.concatenate(
                        [held[q][0], nre], axis=0).astype(BF16)
                    sb_scr[g, p0:p0 + 2 * S5_BATCH, im0:im0 + S5_BLOCK_STATE] = jnp.concatenate(
                        [held[q][1], nim], axis=0).astype(BF16)
        for q in qs:
            re0 = q * S5_BLOCK_COLS
            s_scr[g, :, re0:re0 + S5_BLOCK_STATE] = st[q][0]
            s_scr[g, :, re0 + S5_BLOCK_STATE:re0 + S5_BLOCK_COLS] = st[q][1]

    def cs(g, h):
        return jnp.dot(sb_scr[g, :, h * pair_cols:(h + 1) * pair_cols], c2_ref[h],
                       preferred_element_type=F32)

    def gelu_in(g, ys):
        return jax.nn.gelu(jnp.concatenate(ys, axis=-1) + d_ref[...] * us[g])

    def glu_out(g, z):
        gate = jax.nn.sigmoid(_dot(z, wglu_ref[...]) + bglu_ref[...])
        y_ref[:, g * S5_BATCH:(g + 1) * S5_BATCH, :] = (z * gate).reshape(T, S5_BATCH, D_S5)

    n_pairs = S5_BLOCKS // 2
    for h in range(n_pairs):
        bu(0, h)
    ys = {g: [] for g in range(NG)}
    for g in range(NG):
        for h in range(n_pairs):
            scan(g, h)
            if g + 1 < NG:
                bu(g + 1, h)
            ys[g].append(cs(g, h))
            if g > 0 and h == 0:
                glu_out(g - 1, gelu_in(g - 1, ys[g - 1]))
    glu_out(NG - 1, gelu_in(NG - 1, ys[NG - 1]))


def _s5_params(lam_re, lam_im, log_dt, b_re, b_im, c_re, c_im):
    dt = jnp.exp(log_dt)[:, None]
    mag = jnp.exp(lam_re * dt)
    ab_re = mag * jnp.cos(lam_im * dt)
    ab_im = mag * jnp.sin(lam_im * dt)
    den = lam_re * lam_re + lam_im * lam_im
    nr = ab_re - 1.0
    f_re = (nr * lam_re + ab_im * lam_im) / den
    f_im = (ab_im * lam_re - nr * lam_im) / den
    bb_re = f_re[..., None] * b_re - f_im[..., None] * b_im
    bb_im = f_re[..., None] * b_im + f_im[..., None] * b_re
    eye = jnp.eye(S5_BLOCK_GROUPS, dtype=F32)

    def blk(t, q):
        return t[q * S5_BLOCK_GROUPS:(q + 1) * S5_BLOCK_GROUPS]

    def b_block(bb, q):
        return jnp.einsum("gpc,gk->gckp", blk(bb, q), eye).reshape(LANES, S5_BLOCK_STATE)

    def c_block(cc, q):
        return jnp.einsum("gcp,gk->gpkc", blk(cc, q), eye).reshape(S5_BLOCK_STATE, LANES)

    b4 = jnp.stack([jnp.concatenate([b_block(bb_re, q), b_block(bb_im, q)], axis=1)
                    for q in range(S5_BLOCKS)]).astype(BF16)
    c4 = [jnp.concatenate([c_block(c_re, q), -c_block(c_im, q)], axis=0)
          for q in range(S5_BLOCKS)]
    zc = jnp.zeros_like(c4[0])
    c2 = jnp.stack([jnp.concatenate([jnp.concatenate([c4[2 * h], zc], axis=1),
                                     jnp.concatenate([zc, c4[2 * h + 1]], axis=1)], axis=0)
                    for h in range(S5_BLOCKS // 2)]).astype(BF16)
    return ab_re.reshape(1, -1), ab_im.reshape(1, -1), b4, c2


def _s5(u_tm, a_re, a_im, b4, c2, d_skip, w_glu, b_glu):
    L, B, _ = u_tm.shape
    T = min(S5_T, L)
    NG = min(S5_GROUPS_PER_STEP, B // S5_BATCH)
    n_state = S5_BLOCKS * S5_BLOCK_COLS
    const = lambda *shape: pl.BlockSpec(shape, lambda b, t: (0,) * len(shape))
    return pl.pallas_call(
        functools.partial(_s5_kernel, T=T, NG=NG),
        grid=(B // (NG * S5_BATCH), L // T),
        in_specs=[
            pl.BlockSpec((T, NG * S5_BATCH, D_S5), lambda b, t: (t, b, 0)),
            const(1, S5_GROUPS * S5_STATE),
            const(1, S5_GROUPS * S5_STATE),
            const(S5_BLOCKS, LANES, S5_BLOCK_COLS),
            const(S5_BLOCKS // 2, 2 * S5_BLOCK_COLS, 2 * LANES),
            const(1, D_S5),
            const(D_S5, D_S5),
            const(1, D_S5),
        ],
        out_specs=pl.BlockSpec((T, NG * S5_BATCH, D_S5), lambda b, t: (t, b, 0)),
        out_shape=jax.ShapeDtypeStruct((L, B, D_S5), F32),
        scratch_shapes=[
            pltpu.VMEM((NG, S5_BATCH, n_state), F32),
            pltpu.VMEM((NG, T * S5_BATCH, n_state), F32),
            pltpu.VMEM((NG, T * S5_BATCH, n_state), BF16),
        ],
        compiler_params=pltpu.CompilerParams(
            dimension_semantics=("parallel", "arbitrary"), vmem_limit_bytes=VMEM_LIMIT),
        name="s5",
    )(u_tm, a_re, a_im, b4, c2, d_skip, w_glu, b_glu)


def _rwkv_kernel(q_ref, w0_ref, wwa_ref, a0_ref, gup_ref, kk_ref, ka_ref, rk_ref,
                 gnw_ref, gnb_ref, o_ref, s_scr, lhs_scr, rhs_scr, v_scr, bk_scr, we_scr,
                 bonus_scr, g_scr, *, C, Bt):
    @pl.when(pl.program_id(1) == 0)
    def _():
        for ref in (s_scr, lhs_scr, rhs_scr, v_scr, bk_scr, we_scr, bonus_scr, g_scr):
            ref[...] = jnp.zeros_like(ref)

    row = lax.broadcasted_iota(jnp.int32, (C, LANES), 0)
    lane = lax.broadcasted_iota(jnp.int32, (C, LANES), 1)
    col = lane & (RWKV_HEAD - 1)
    head0 = lane < RWKV_HEAD
    strict = col < row
    incl = col <= row
    eye = (col == row).astype(F32)
    blk = row ^ col
    r128 = lax.broadcasted_iota(jnp.int32, (LANES, LANES), 0)
    c128 = lax.broadcasted_iota(jnp.int32, (LANES, LANES), 1)
    same_head = (r128 < RWKV_HEAD) == (c128 < RWKV_HEAD)
    ones_blk = same_head.astype(BF16)
    tril = (lax.broadcasted_iota(jnp.int32, (C, C), 0)
            >= lax.broadcasted_iota(jnp.int32, (C, C), 1)).astype(BF16)
    rows = Bt * C
    sl = [slice(p * LANES, (p + 1) * LANES) for p in range(RW_PAIRS)]
    bis = list(range(Bt))
    ch = [(bi, p) for bi in bis for p in range(RW_PAIRS)]
    n_ch = len(ch)

    def bd(y):
        zero = jnp.zeros_like(y)
        return jnp.concatenate([jnp.where(head0, y, zero), jnp.where(head0, zero, y)], axis=0)

    def bmm(x, y):
        return _dot(x, bd(y))

    def headsum(x):
        n = x.shape[0]
        tall = jnp.concatenate([x[:, s_] for s_ in sl], axis=0)
        t = jnp.dot(tall.astype(BF16), ones_blk, preferred_element_type=F32)
        return jnp.concatenate([t[p * n:(p + 1) * n] for p in range(RW_PAIRS)], axis=-1)

    nt = (((1,), (1,)), ((), ()))
    tn = (((0,), (0,)), ((), ()))

    def each(fn, *lists):
        return [fn(*args) for args in zip(*lists)]

    lhs = [lhs_scr[i] for i in range(n_ch)]
    rhs = [rhs_scr[i] for i in range(n_ch)]
    vp = [v_scr[i] for i in range(n_ch)]
    bk = [bk_scr[i] for i in range(n_ch)]
    wep = [we_scr[bi][:, sl[p]] for bi, p in ch]
    S = [s_scr[bi, p] for bi, p in ch]
    bonus_prev = bonus_scr[...]
    g_prev = g_scr[...]

    zero = jnp.zeros((C, LANES), BF16)
    lhs_m = each(lambda t: jnp.concatenate(
        [jnp.where(head0, t[:C], zero), jnp.where(head0, zero, t[:C]),
         jnp.where(head0, t[C:], zero), jnp.where(head0, zero, t[C:])], axis=0), lhs)
    gram = each(lambda x, y: lax.dot_general(x, y, nt, preferred_element_type=F32), lhs_m, rhs)
    swap = lambda t: pltpu.roll(t, RWKV_HEAD, 1)
    a_ab = [jnp.where(strict, jnp.where(head0, t[:C], swap(t[C:2 * C])), 0.0) for t in gram]
    a_ak = [jnp.where(strict, jnp.where(head0, swap(t[:C]), t[C:2 * C]), 0.0) for t in gram]
    r_b = [jnp.where(incl, jnp.where(head0, t[2 * C:3 * C], swap(t[3 * C:])), 0.0) for t in gram]
    r_k = [jnp.where(incl, jnp.where(head0, swap(t[2 * C:3 * C]), t[3 * C:]), 0.0) for t in gram]
    ss = each(lambda x, st: lax.dot_general(x, st.astype(BF16), nt, preferred_element_type=F32),
              lhs, S)
    akv = each(lambda aa, rk_, vv: bmm(jnp.concatenate([aa, rk_], axis=0), vv), a_ak, r_k, vp)
    pm = [jnp.where(blk < 8, t, 0.0) for t in a_ab]
    p2 = each(bmm, pm, pm)

    qs = q_ref[...].reshape(rows, N_RW_COLS)
    r = qs[:, 0:D_RWKV]
    k = qs[:, D_RWKV:2 * D_RWKV]
    v = qs[:, 2 * D_RWKV:3 * D_RWKV]
    xwa = qs[:, 3 * D_RWKV:3 * D_RWKV + LANES]
    head0r = lax.broadcasted_iota(jnp.int32, (rows, LANES), 1) < RWKV_HEAD
    wa = _dot(jnp.where(head0r, jnp.tanh(xwa), xwa), wwa_ref[...])
    g = _dot(jax.nn.sigmoid(qs[:, 3 * D_RWKV + LANES:]), gup_ref[...])
    kk = k * kk_ref[...]
    kss = headsum(kk * kk)

    ip = [eye + t for t in pm]
    both = each(lambda x, y: bmm(jnp.concatenate([x, y], axis=0), x), p2, ip)
    t_inv = each(lambda x, y: x + y[C:], ip, both)
    t_inv = each(lambda x, y: x + bmm(x, y[:C]), t_inv, both)

    def double(t_in, gsz):
        m = [jnp.where((blk >= gsz) & (blk < 2 * gsz), t, 0.0) for t in a_ab]
        tm = each(bmm, t_in, m)
        return each(lambda x, y: x + bmm(y, x), t_in, tm)

    t_inv = double(t_inv, 8)

    lw = -math.exp(-0.5) * jax.nn.sigmoid(w0_ref[...] + wa[:, :D_RWKV])
    a = jax.nn.sigmoid(a0_ref[...] + wa[:, D_RWKV:])
    kk = kk / jnp.maximum(jnp.sqrt(kss), 1e-12)
    k2 = k * (1.0 + (a - 1.0) * ka_ref[...])
    bonus = headsum(r * k2 * rk_ref[...]) * v
    lw_hi = lw.astype(BF16)
    lw_lo = (lw - lw_hi.astype(F32)).astype(BF16)
    cum = jnp.concatenate(
        [jnp.dot(tril, lw_hi[bi * C:(bi + 1) * C], preferred_element_type=F32)
         + jnp.dot(tril, lw_lo[bi * C:(bi + 1) * C], preferred_element_type=F32) for bi in bis],
        axis=0)

    gsz = 16
    while gsz < C:
        t_inv = double(t_inv, gsz)
        gsz *= 2
    x = each(lambda s_, t: s_[:C] + t[:C], ss, akv)
    u = each(bmm, t_inv, x)
    o = each(lambda s_, t, rb, uu: s_[C:] + t[C:] + bmm(rb, uu), ss, akv, r_b, u)
    upd = each(lambda uu, vv, bb: lax.dot_general(
        jnp.concatenate([uu.astype(BF16), vv], axis=0), bb, tn, preferred_element_type=F32),
        u, vp, bk)
    for (bi, p), st, we, up in zip(ch, S, wep, upd):
        s_scr[bi, p] = st * we + jnp.where(same_head, up, 0.0)
    o = jnp.concatenate(
        [jnp.concatenate(o[bi * RW_PAIRS:(bi + 1) * RW_PAIRS], axis=-1) for bi in bis], axis=0)
    dlt = o - headsum(o) * (1.0 / RWKV_HEAD)
    var = headsum(dlt * dlt) * (1.0 / RWKV_HEAD)
    on = dlt * lax.rsqrt(var + GN_EPS) * gnw_ref[...] + gnb_ref[...]
    o_ref[...] = ((on + bonus_prev) * g_prev).reshape(Bt, C, D_RWKV)

    w_in = jnp.exp(cum)
    w_inv = jnp.exp(-cum)
    ah = -kk * jnp.exp(cum - lw)
    bh = kk * a * w_inv
    kh = k2 * w_inv
    rh = r * w_in

    for i, (bi, p) in enumerate(ch):
        rs = slice(bi * C, (bi + 1) * C)
        we = w_in[(bi + 1) * C - 1:(bi + 1) * C, sl[p]]
        lhs_scr[i] = jnp.concatenate([ah[rs, sl[p]], rh[rs, sl[p]]], axis=0).astype(BF16)
        rhs_scr[i] = jnp.concatenate([bh[rs, sl[p]], kh[rs, sl[p]]], axis=0).astype(BF16)
        bk_scr[i] = jnp.concatenate([bh[rs, sl[p]] * we, kh[rs, sl[p]] * we], axis=0).astype(BF16)
        v_scr[i] = v[rs, sl[p]].astype(BF16)
    for bi in bis:
        we_scr[bi] = w_in[(bi + 1) * C - 1:(bi + 1) * C, :]
    bonus_scr[...] = bonus
    g_scr[...] = g


def _rwkv(q, w0, wwa, a0, g_up, k_k, k_a, r_k, gn_w, gn_b, *, Bt=RW_BT):
    B, L, _ = q.shape
    C = RW_CHUNK
    n_chunks = L // C
    n_ch = Bt * RW_PAIRS
    const = lambda *shape: pl.BlockSpec(shape, lambda b, c: (0,) * len(shape))
    return pl.pallas_call(
        functools.partial(_rwkv_kernel, C=C, Bt=Bt),
        grid=(B // Bt, n_chunks + 1),
        in_specs=[
            pl.BlockSpec((Bt, C, N_RW_COLS), lambda b, c: (b, jnp.minimum(c, n_chunks - 1), 0)),
            const(1, D_RWKV),
            const(LANES, 2 * D_RWKV),
            const(1, D_RWKV),
            const(G_RANK, D_RWKV),
            const(1, D_RWKV),
            const(1, D_RWKV),
            const(1, D_RWKV),
            const(1, D_RWKV),
            const(1, D_RWKV),
        ],
        out_specs=pl.BlockSpec((Bt, C, D_RWKV), lambda b, c: (b, jnp.maximum(c - 1, 0), 0)),
        out_shape=jax.ShapeDtypeStruct((B, L, D_RWKV), F32),
        scratch_shapes=[
            pltpu.VMEM((Bt, RW_PAIRS, LANES, LANES), F32),
            pltpu.VMEM((n_ch, 2 * C, LANES), BF16),
            pltpu.VMEM((n_ch, 2 * C, LANES), BF16),
            pltpu.VMEM((n_ch, C, LANES), BF16),
            pltpu.VMEM((n_ch, 2 * C, LANES), BF16),
            pltpu.VMEM((Bt, 1, D_RWKV), F32),
            pltpu.VMEM((Bt * C, D_RWKV), F32),
            pltpu.VMEM((Bt * C, D_RWKV), F32),
        ],
        compiler_params=pltpu.CompilerParams(
            dimension_semantics=("parallel", "arbitrary"), vmem_limit_bytes=VMEM_LIMIT),
        name="rwkv7",
    )(q, w0, wwa, a0, g_up, k_k, k_a, r_k, gn_w, gn_b)


def _ffn_kernel(ys_ref, yr_ref, x_ref, wo_ref, gpm_ref, gpre_ref, wup_ref, wdn_ref, gpost_ref,
                o_ref):
    mix = _dot(ys_ref[...], wo_ref[:D_S5, :]) + _dot(yr_ref[0], wo_ref[D_S5:, :])
    h = x_ref[0] + _rms(mix, gpm_ref[...])
    hn = _rms(h, gpre_ref[...])
    up = _dot(hn, wup_ref[...])
    ff = _dot(jnp.square(jnp.maximum(up, 0.0)), wdn_ref[...])
    o_ref[0] = h + _rms(ff, gpost_ref[...])


def _ffn(ys_tm, yr, x, w_out, g_post_mix, g_pre_mlp, w_up, w_down, g_post_mlp):
    B, L, _ = x.shape
    tl = min(ROW_TILE, L)
    const = lambda *shape: pl.BlockSpec(shape, lambda b, i: (0,) * len(shape),
                                        pipeline_mode=pl.Buffered(1))
    return pl.pallas_call(
        _ffn_kernel,
        grid=(B, L // tl),
        in_specs=[
            pl.BlockSpec((tl, D_S5), lambda b, i: (i, b)),
            pl.BlockSpec((1, tl, D_RWKV), lambda b, i: (b, i, 0)),
            pl.BlockSpec((1, tl, D_MODEL), lambda b, i: (b, i, 0)),
            const(D_MODEL, D_MODEL),
            const(1, D_MODEL),
            const(1, D_MODEL),
            const(D_MODEL, D_FF),
            const(D_FF, D_MODEL),
            const(1, D_MODEL),
        ],
        out_specs=pl.BlockSpec((1, tl, D_MODEL), lambda b, i: (b, i, 0)),
        out_shape=jax.ShapeDtypeStruct((B, L, D_MODEL), F32),
        compiler_params=pltpu.CompilerParams(
            dimension_semantics=("parallel", "parallel"), vmem_limit_bytes=VMEM_LIMIT),
        name="ffn",
    )(ys_tm.reshape(L, B * D_S5), yr, x, w_out, g_post_mix, g_pre_mlp, w_up, w_down, g_post_mlp)


def _layer(x, g_pre_mix, w_in, s5_lam_re, s5_lam_im, s5_log_dt, s5_b_re, s5_b_im, s5_c_re, s5_c_im,
           s5_d, s5_w_glu, s5_b_glu, rw_mu, rw_w0, rw_w_up, rw_a0, rw_a_up, rw_g_up, rw_k_k, rw_k_a,
           rw_r_k, rw_gn_w, rw_gn_b, w_out, g_post_mix, g_pre_mlp, w_ff_up, w_ff_down, g_post_mlp):
    row = lambda t: t.reshape(1, -1)
    u_tm, q = _inproj(x, row(g_pre_mix), w_in.astype(BF16), row(rw_mu))
    a_re, a_im, b4, c2 = _s5_params(s5_lam_re, s5_lam_im, s5_log_dt, s5_b_re, s5_b_im,
                                    s5_c_re, s5_c_im)
    ys_tm = _s5(u_tm, a_re, a_im, b4, c2, row(s5_d), s5_w_glu.astype(BF16), row(s5_b_glu))
    zeros = jnp.zeros((W_RANK, D_RWKV), F32)
    wwa = jnp.concatenate([jnp.concatenate([rw_w_up, zeros], axis=1),
                           jnp.concatenate([zeros, rw_a_up], axis=1)], axis=0).astype(BF16)
    yr = _rwkv(q, row(rw_w0), wwa, row(rw_a0), rw_g_up.astype(BF16), row(rw_k_k),
               row(rw_k_a), row(rw_r_k), row(rw_gn_w), row(rw_gn_b))
    return _ffn(ys_tm, yr, x, w_out.astype(BF16), row(g_post_mix), row(g_pre_mlp),
                w_ff_up.astype(BF16), w_ff_down.astype(BF16), row(g_post_mlp))


def kernel(x, g_pre_mix, w_in, s5_lam_re, s5_lam_im, s5_log_dt, s5_b_re, s5_b_im, s5_c_re, s5_c_im,
           s5_d, s5_w_glu, s5_b_glu, rw_mu, rw_w0, rw_w_up, rw_a0, rw_a_up, rw_g_up, rw_k_k, rw_k_a,
           rw_r_k, rw_gn_w, rw_gn_b, w_out, g_post_mix, g_pre_mlp, w_ff_up, w_ff_down, g_post_mlp):
    params = (g_pre_mix, w_in, s5_lam_re, s5_lam_im, s5_log_dt, s5_b_re, s5_b_im, s5_c_re, s5_c_im,
              s5_d, s5_w_glu, s5_b_glu, rw_mu, rw_w0, rw_w_up, rw_a0, rw_a_up, rw_g_up, rw_k_k,
              rw_k_a, rw_r_k, rw_gn_w, rw_gn_b, w_out, g_post_mix, g_pre_mlp, w_ff_up, w_ff_down,
              g_post_mlp)
    h = x
    for layer in range(g_pre_mix.shape[0]):
        h = _layer(h, *[p[layer] for p in params])
    return h
```

```python
import functools
import math

import jax
import jax.numpy as jnp
from jax import lax
from jax.experimental import pallas as pl
from jax.experimental.pallas import tpu as pltpu

F32 = jnp.float32
BF16 = jnp.bfloat16

D_MODEL = 1024
D_S5 = 512
D_RWKV = 512
S5_GROUP = 16
S5_GROUPS = 32
S5_STATE = 64
RWKV_HEAD = 64
RWKV_HEADS = 8
W_RANK = 64
A_RANK = 64
G_RANK = 128
N_RW_COLS = 3 * D_RWKV + W_RANK + A_RANK + G_RANK
N_PROJ = D_S5 + N_RW_COLS
D_FF = 4 * D_MODEL
RMS_EPS = 1e-6
GN_EPS = 64e-5

SUBLANES = 8
LANES = 128

ROW_TILE = 512
IN_BATCH = 16
S5_BATCH = SUBLANES
S5_T = 64
S5_GROUPS_PER_STEP = 2
S5_BLOCK_GROUPS = LANES // S5_GROUP
S5_BLOCKS = S5_GROUPS // S5_BLOCK_GROUPS
S5_BLOCK_STATE = S5_BLOCK_GROUPS * S5_STATE
S5_BLOCK_COLS = 2 * S5_BLOCK_STATE
RW_CHUNK = 64
RW_PAIRS = RWKV_HEADS // 2
RW_BT = 8
VMEM_LIMIT = 56 * 1024 * 1024


def _rms(x, g):
    return x * lax.rsqrt(jnp.mean(x * x, axis=-1, keepdims=True) + RMS_EPS) * g


def _dot(a, b):
    return jnp.dot(a.astype(BF16), b.astype(BF16), preferred_element_type=F32)


def _pair_ones():
    r = lax.broadcasted_iota(jnp.int32, (LANES, LANES), 0)
    c = lax.broadcasted_iota(jnp.int32, (LANES, LANES), 1)
    return ((r < RWKV_HEAD) == (c < RWKV_HEAD)).astype(BF16)


def _headsum(x, ones_blk):
    n = x.shape[0]
    tall = jnp.concatenate([x[:, p * LANES:(p + 1) * LANES] for p in range(RW_PAIRS)], axis=0)
    t = jnp.dot(tall.astype(BF16), ones_blk, preferred_element_type=F32)
    return jnp.concatenate([t[p * n:(p + 1) * n] for p in range(RW_PAIRS)], axis=-1)


def _inproj_kernel(x_ref, g_ref, w_ref, mu_ref, w0_ref, wwa_ref, a0_ref, gup_ref, kk_ref, ka_ref,
                   rk_ref, u_ref, ah_ref, rh_ref, bh_ref, kh_ref, v_ref, we_ref, bonus_ref, gate_ref,
                   prev_scr, *, nb, T):
    @pl.when(pl.program_id(1) == 0)
    def _():
        prev_scr[...] = jnp.zeros_like(prev_scr)

    rows = nb * T
    xn = _rms(x_ref[...].reshape(rows, D_MODEL), g_ref[...]).astype(BF16)
    first = lax.broadcasted_iota(jnp.int32, (SUBLANES, LANES), 0) == 0

    def project(c0, c1):
        q = jnp.dot(xn, w_ref[:, D_S5 + c0:D_S5 + c1], preferred_element_type=F32)
        rolled = pltpu.roll(q, 1, 0)
        edge = jnp.concatenate([first] * ((c1 - c0) // LANES), axis=1)
        pieces = []
        for b in range(nb):
            pieces.append(jnp.where(edge, prev_scr[b, :, c0:c1], rolled[b * T:b * T + SUBLANES]))
            pieces.append(rolled[b * T + SUBLANES:(b + 1) * T])
            prev_scr[b, :, c0:c1] = q[(b + 1) * T - 1:(b + 1) * T, :]
        prev = jnp.concatenate(pieces, axis=0)
        return q + mu_ref[:, c0:c1] * (prev - q)

    ones_blk = _pair_ones()
    xw_xa_xg = project(3 * D_RWKV, N_RW_COLS)
    xwa = xw_xa_xg[:, :LANES]
    k = project(D_RWKV, 2 * D_RWKV)
    head0r = lax.broadcasted_iota(jnp.int32, (rows, LANES), 1) < RWKV_HEAD
    wa = _dot(jnp.where(head0r, jnp.tanh(xwa), xwa), wwa_ref[...])
    gate = _dot(jax.nn.sigmoid(xw_xa_xg[:, LANES:]), gup_ref[...])
    r = project(0, D_RWKV)
    kk = k * kk_ref[...]
    kk = kk / jnp.maximum(jnp.sqrt(_headsum(kk * kk, ones_blk)), 1e-12)
    v = project(2 * D_RWKV, 3 * D_RWKV)
    lw = -math.exp(-0.5) * jax.nn.sigmoid(w0_ref[...] + wa[:, :D_RWKV])
    a = jax.nn.sigmoid(a0_ref[...] + wa[:, D_RWKV:])
    k2 = k * (1.0 + (a - 1.0) * ka_ref[...])
    tril = (lax.broadcasted_iota(jnp.int32, (T, T), 0)
            >= lax.broadcasted_iota(jnp.int32, (T, T), 1)).astype(BF16)
    lw_hi = lw.astype(BF16)
    lw_lo = (lw - lw_hi.astype(F32)).astype(BF16)
    cum = jnp.concatenate(
        [jnp.dot(tril, lw_hi[b * T:(b + 1) * T], preferred_element_type=F32)
         + jnp.dot(tril, lw_lo[b * T:(b + 1) * T], preferred_element_type=F32) for b in range(nb)],
        axis=0)
    u = jnp.dot(xn, w_ref[:, :D_S5], preferred_element_type=F32)
    for b in range(nb):
        u_ref[:, b, :] = u[b * T:(b + 1) * T, :]
    bonus = _headsum(r * k2 * rk_ref[...], ones_blk) * v
    w_in = jnp.exp(cum)
    w_inv = jnp.exp(-cum)
    shape = (nb, T, D_RWKV)
    ah_ref[...] = (-kk * jnp.exp(cum - lw)).astype(BF16).reshape(shape)
    rh_ref[...] = (r * w_in).astype(BF16).reshape(shape)
    bh_ref[...] = (kk * a * w_inv).astype(BF16).reshape(shape)
    kh_ref[...] = (k2 * w_inv).astype(BF16).reshape(shape)
    v_ref[...] = v.astype(BF16).reshape(shape)
    bonus_ref[...] = bonus.reshape(shape)
    gate_ref[...] = gate.reshape(shape)
    for b in range(nb):
        we_ref[b, 0] = w_in[(b + 1) * T - 1:(b + 1) * T, :]


def _inproj(x, g, w_in, mu, w0, wwa, a0, g_up, k_k, k_a, r_k):
    B, L, _ = x.shape
    nb = min(IN_BATCH, B)
    T = RW_CHUNK
    const = lambda *shape: pl.BlockSpec(shape, lambda b, i: (0,) * len(shape))
    seq = lambda: pl.BlockSpec((nb, T, D_RWKV), lambda b, i: (b, i, 0))
    bf = jax.ShapeDtypeStruct((B, L, D_RWKV), BF16)
    f32 = jax.ShapeDtypeStruct((B, L, D_RWKV), F32)
    return pl.pallas_call(
        functools.partial(_inproj_kernel, nb=nb, T=T),
        grid=(B // nb, L // T),
        in_specs=[
            pl.BlockSpec((nb, T, D_MODEL), lambda b, i: (b, i, 0)),
            const(1, D_MODEL),
            const(D_MODEL, N_PROJ),
            const(1, N_RW_COLS),
            const(1, D_RWKV),
            const(LANES, 2 * D_RWKV),
            const(1, D_RWKV),
            const(G_RANK, D_RWKV),
            const(1, D_RWKV),
            const(1, D_RWKV),
            const(1, D_RWKV),
        ],
        out_specs=[
            pl.BlockSpec((T, nb, D_S5), lambda b, i: (i, b, 0)),
            seq(), seq(), seq(), seq(), seq(),
            pl.BlockSpec((nb, 1, 1, D_RWKV), lambda b, i: (b, i, 0, 0)),
            seq(), seq(),
        ],
        out_shape=[
            jax.ShapeDtypeStruct((L, B, D_S5), F32),
            bf, bf, bf, bf, bf,
            jax.ShapeDtypeStruct((B, L // T, 1, D_RWKV), F32),
            f32, f32,
        ],
        scratch_shapes=[pltpu.VMEM((nb, 1, N_RW_COLS), F32)],
        compiler_params=pltpu.CompilerParams(
            dimension_semantics=("parallel", "arbitrary"), vmem_limit_bytes=VMEM_LIMIT),
        name="inproj",
    )(x, g, w_in, mu, w0, wwa, a0, g_up, k_k, k_a, r_k)


def _s5_kernel(u_ref, are_ref, aim_ref, b4_ref, c2_ref, d_ref, wglu_ref, bglu_ref,
               y_ref, s_scr, bu_scr, sb_scr, *, T, NG):
    rows = T * S5_BATCH
    pair_cols = 2 * S5_BLOCK_COLS

    @pl.when(pl.program_id(1) == 0)
    def _():
        s_scr[...] = jnp.zeros_like(s_scr)

    us = [u_ref[:, g * S5_BATCH:(g + 1) * S5_BATCH, :].reshape(rows, D_S5) for g in range(NG)]
    ubs = [u.astype(BF16) for u in us]

    def bu(g, h):
        for q in (2 * h, 2 * h + 1):
            bu_scr[g, :, q * S5_BLOCK_COLS:(q + 1) * S5_BLOCK_COLS] = jnp.dot(
                ubs[g][:, q * LANES:(q + 1) * LANES], b4_ref[q], preferred_element_type=F32)

    def scan(g, h):
        qs = (2 * h, 2 * h + 1)
        a_bc = {}
        st = {}
        for q in qs:
            a0 = q * S5_BLOCK_STATE
            a_bc[q] = (
                jnp.broadcast_to(are_ref[:, a0:a0 + S5_BLOCK_STATE], (S5_BATCH, S5_BLOCK_STATE)),
                jnp.broadcast_to(aim_ref[:, a0:a0 + S5_BLOCK_STATE], (S5_BATCH, S5_BLOCK_STATE)))
            re0 = q * S5_BLOCK_COLS
            st[q] = (s_scr[g, :, re0:re0 + S5_BLOCK_STATE],
                     s_scr[g, :, re0 + S5_BLOCK_STATE:re0 + S5_BLOCK_COLS])
        held = {}
        for t in range(T):
            r0 = t * S5_BATCH
            for q in qs:
                re0 = q * S5_BLOCK_COLS
                im0 = re0 + S5_BLOCK_STATE
                are, aim = a_bc[q]
                sre, sim = st[q]
                nre = are * sre - aim * sim + bu_scr[g, r0:r0 + S5_BATCH, re0:re0 + S5_BLOCK_STATE]
                nim = are * sim + aim * sre + bu_scr[g, r0:r0 + S5_BATCH, im0:im0 + S5_BLOCK_STATE]
                st[q] = (nre, nim)
                if t % 2 == 0:
                    held[q] = (nre, nim)
                else:
                    p0 = r0 - S5_BATCH
                    sb_scr[g, p0:p0 + 2 * S5_BATCH, re0:re0 + S5_BLOCK_STATE] = jnp.concatenate(
                        [held[q][0], nre], axis=0).astype(BF16)
                    sb_scr[g, p0:p0 + 2 * S5_BATCH, im0:im0 + S5_BLOCK_STATE] = jnp.concatenate(
                        [held[q][1], nim], axis=0).astype(BF16)
        for q in qs:
            re0 = q * S5_BLOCK_COLS
            s_scr[g, :, re0:re0 + S5_BLOCK_STATE] = st[q][0]
            s_scr[g, :, re0 + S5_BLOCK_STATE:re0 + S5_BLOCK_COLS] = st[q][1]

    def cs(g, h):
        return jnp.dot(sb_scr[g, :, h * pair_cols:(h + 1) * pair_cols], c2_ref[h],
                       preferred_element_type=F32)

    def gelu_in(g, ys):
        return jax.nn.gelu(jnp.concatenate(ys, axis=-1) + d_ref[...] * us[g])

    def glu_out(g, z):
        gate = jax.nn.sigmoid(_dot(z, wglu_ref[...]) + bglu_ref[...])
        y_ref[:, g * S5_BATCH:(g + 1) * S5_BATCH, :] = (z * gate).reshape(T, S5_BATCH, D_S5)

    n_pairs = S5_BLOCKS // 2
    for h in range(n_pairs):
        bu(0, h)
    ys = {g: [] for g in range(NG)}
    for g in range(NG):
        for h in range(n_pairs):
            scan(g, h)
            if g + 1 < NG:
                bu(g + 1, h)
            ys[g].append(cs(g, h))
            if g > 0 and h == 0:
                glu_out(g - 1, gelu_in(g - 1, ys[g - 1]))
    glu_out(NG - 1, gelu_in(NG - 1, ys[NG - 1]))


def _s5_params(lam_re, lam_im, log_dt, b_re, b_im, c_re, c_im):
    dt = jnp.exp(log_dt)[:, None]
    mag = jnp.exp(lam_re * dt)
    ab_re = mag * jnp.cos(lam_im * dt)
    ab_im = mag * jnp.sin(lam_im * dt)
    den = lam_re * lam_re + lam_im * lam_im
    nr = ab_re - 1.0
    f_re = (nr * lam_re + ab_im * lam_im) / den
    f_im = (ab_im * lam_re - nr * lam_im) / den
    bb_re = f_re[..., None] * b_re - f_im[..., None] * b_im
    bb_im = f_re[..., None] * b_im + f_im[..., None] * b_re
    eye = jnp.eye(S5_BLOCK_GROUPS, dtype=F32)

    def blk(t, q):
        return t[q * S5_BLOCK_GROUPS:(q + 1) * S5_BLOCK_GROUPS]

    def b_block(bb, q):
        return jnp.einsum("gpc,gk->gckp", blk(bb, q), eye).reshape(LANES, S5_BLOCK_STATE)

    def c_block(cc, q):
        return jnp.einsum("gcp,gk->gpkc", blk(cc, q), eye).reshape(S5_BLOCK_STATE, LANES)

    b4 = jnp.stack([jnp.concatenate([b_block(bb_re, q), b_block(bb_im, q)], axis=1)
                    for q in range(S5_BLOCKS)]).astype(BF16)
    c4 = [jnp.concatenate([c_block(c_re, q), -c_block(c_im, q)], axis=0)
          for q in range(S5_BLOCKS)]
    zc = jnp.zeros_like(c4[0])
    c2 = jnp.stack([jnp.concatenate([jnp.concatenate([c4[2 * h], zc], axis=1),
                                     jnp.concatenate([zc, c4[2 * h + 1]], axis=1)], axis=0)
                    for h in range(S5_BLOCKS // 2)]).astype(BF16)
    return ab_re.reshape(1, -1), ab_im.reshape(1, -1), b4, c2


def _s5(u_tm, a_re, a_im, b4, c2, d_skip, w_glu, b_glu):
    L, B, _ = u_tm.shape
    T = min(S5_T, L)
    NG = min(S5_GROUPS_PER_STEP, B // S5_BATCH)
    n_state = S5_BLOCKS * S5_BLOCK_COLS
    const = lambda *shape: pl.BlockSpec(shape, lambda b, t: (0,) * len(shape))
    return pl.pallas_call(
        functools.partial(_s5_kernel, T=T, NG=NG),
        grid=(B // (NG * S5_BATCH), L // T),
        in_specs=[
            pl.BlockSpec((T, NG * S5_BATCH, D_S5), lambda b, t: (t, b, 0)),
            const(1, S5_GROUPS * S5_STATE),
            const(1, S5_GROUPS * S5_STATE),
            const(S5_BLOCKS, LANES, S5_BLOCK_COLS),
            const(S5_BLOCKS // 2, 2 * S5_BLOCK_COLS, 2 * LANES),
            const(1, D_S5),
            const(D_S5, D_S5),
            const(1, D_S5),
        ],
        out_specs=pl.BlockSpec((T, NG * S5_BATCH, D_S5), lambda b, t: (t, b, 0)),
        out_shape=jax.ShapeDtypeStruct((L, B, D_S5), F32),
        scratch_shapes=[
            pltpu.VMEM((NG, S5_BATCH, n_state), F32),
            pltpu.VMEM((NG, T * S5_BATCH, n_state), F32),
            pltpu.VMEM((NG, T * S5_BATCH, n_state), BF16),
        ],
        compiler_params=pltpu.CompilerParams(
            dimension_semantics=("parallel", "arbitrary"), vmem_limit_bytes=VMEM_LIMIT),
        name="s5",
    )(u_tm, a_re, a_im, b4, c2, d_skip, w_glu, b_glu)


def _rwkv_kernel(ah_ref, rh_ref, bh_ref, kh_ref, v_ref, we_ref, bonus_ref, gate_ref, gnw_ref,
                 gnb_ref, o_ref, s_scr, *, C, Bt):
    @pl.when(pl.program_id(1) == 0)
    def _():
        s_scr[...] = jnp.zeros_like(s_scr)

    row = lax.broadcasted_iota(jnp.int32, (C, LANES), 0)
    lane = lax.broadcasted_iota(jnp.int32, (C, LANES), 1)
    col = lane & (RWKV_HEAD - 1)
    head0 = lane < RWKV_HEAD
    strict = col < row
    incl = col <= row
    eye = (col == row).astype(F32)
    blk = row ^ col
    ones_blk = _pair_ones()
    same_head = ones_blk > 0
    sl = [slice(p * LANES, (p + 1) * LANES) for p in range(RW_PAIRS)]
    bis = list(range(Bt))
    ch = [(bi, p) for bi in bis for p in range(RW_PAIRS)]

    def bd(y):
        zero = jnp.zeros_like(y)
        return jnp.concatenate([jnp.where(head0, y, zero), jnp.where(head0, zero, y)], axis=0)

    def bmm(x, y):
        return _dot(x, bd(y))

    nt = (((1,), (1,)), ((), ()))
    tn = (((0,), (0,)), ((), ()))

    def each(fn, *lists):
        return [fn(*args) for args in zip(*lists)]

    ahp = [ah_ref[bi][:, sl[p]] for bi, p in ch]
    rhp = [rh_ref[bi][:, sl[p]] for bi, p in ch]
    bhp = [bh_ref[bi][:, sl[p]] for bi, p in ch]
    khp = [kh_ref[bi][:, sl[p]] for bi, p in ch]
    vp = [v_ref[bi][:, sl[p]] for bi, p in ch]
    wep = [we_ref[bi, 0][:, sl[p]] for bi, p in ch]
    S = [s_scr[bi, p] for bi, p in ch]

    zero = jnp.zeros((C, LANES), BF16)
    lhs = each(lambda x, y: jnp.concatenate([x, y], axis=0), ahp, rhp)
    lhs_m = each(lambda x, y: jnp.concatenate(
        [jnp.where(head0, x, zero), jnp.where(head0, zero, x),
         jnp.where(head0, y, zero), jnp.where(head0, zero, y)], axis=0), ahp, rhp)
    rhs = each(lambda x, y: jnp.concatenate([x, y], axis=0), bhp, khp)
    gram = each(lambda x, y: lax.dot_general(x, y, nt, preferred_element_type=F32), lhs_m, rhs)
    swap = lambda t: pltpu.roll(t, RWKV_HEAD, 1)
    a_ab = [jnp.where(strict, jnp.where(head0, t[:C], swap(t[C:2 * C])), 0.0) for t in gram]
    a_ak = [jnp.where(strict, jnp.where(head0, swap(t[:C]), t[C:2 * C]), 0.0) for t in gram]
    r_b = [jnp.where(incl, jnp.where(head0, t[2 * C:3 * C], swap(t[3 * C:])), 0.0) for t in gram]
    r_k = [jnp.where(incl, jnp.where(head0, swap(t[2 * C:3 * C]), t[3 * C:]), 0.0) for t in gram]
    ss = each(lambda x, st: lax.dot_general(x, st.astype(BF16), nt, preferred_element_type=F32),
              lhs, S)
    akv = each(lambda aa, rk_, vv: bmm(jnp.concatenate([aa, rk_], axis=0), vv), a_ak, r_k, vp)
    pm = [jnp.where(blk < 8, t, 0.0) for t in a_ab]
    p2 = each(bmm, pm, pm)
    ip = [eye + t for t in pm]
    both = each(lambda x, y: bmm(jnp.concatenate([x, y], axis=0), x), p2, ip)
    t_inv = each(lambda x, y: x + y[C:], ip, both)
    t_inv = each(lambda x, y: x + bmm(x, y[:C]), t_inv, both)
    gsz = 8
    while gsz < C:
        m = [jnp.where((blk >= gsz) & (blk < 2 * gsz), t, 0.0) for t in a_ab]
        tm = each(bmm, t_inv, m)
        t_inv = each(lambda x, y: x + bmm(y, x), t_inv, tm)
        gsz *= 2
    x = each(lambda s_, t: s_[:C] + t[:C], ss, akv)
    u = each(bmm, t_inv, x)
    o = each(lambda s_, t, rb, uu: s_[C:] + t[C:] + bmm(rb, uu), ss, akv, r_b, u)
    upd = each(lambda uu, vv, bb, we: lax.dot_general(
        jnp.concatenate([uu.astype(BF16), vv], axis=0), (bb * we).astype(BF16), tn,
        preferred_element_type=F32), u, vp, rhs, wep)
    for (bi, p), st, we, up in zip(ch, S, wep, upd):
        s_scr[bi, p] = st * we + jnp.where(same_head, up, 0.0)
    o = jnp.concatenate(
        [jnp.concatenate(o[bi * RW_PAIRS:(bi + 1) * RW_PAIRS], axis=-1) for bi in bis], axis=0)
    dlt = o - _headsum(o, ones_blk) * (1.0 / RWKV_HEAD)
    var = _headsum(dlt * dlt, ones_blk) * (1.0 / RWKV_HEAD)
    on = dlt * lax.rsqrt(var + GN_EPS) * gnw_ref[...] + gnb_ref[...]
    rows = Bt * C
    out = (on + bonus_ref[...].reshape(rows, D_RWKV)) * gate_ref[...].reshape(rows, D_RWKV)
    o_ref[...] = out.reshape(Bt, C, D_RWKV)


def _rwkv(ah, rh, bh, kh, v, we, bonus, gate, gn_w, gn_b, *, Bt=RW_BT):
    B, L, _ = ah.shape
    C = RW_CHUNK
    Bt = min(Bt, B)
    seq = lambda: pl.BlockSpec((Bt, C, D_RWKV), lambda b, c: (b, c, 0))
    const = lambda *shape: pl.BlockSpec(shape, lambda b, c: (0,) * len(shape))
    return pl.pallas_call(
        functools.partial(_rwkv_kernel, C=C, Bt=Bt),
        grid=(B // Bt, L // C),
        in_specs=[
            seq(), seq(), seq(), seq(), seq(),
            pl.BlockSpec((Bt, 1, 1, D_RWKV), lambda b, c: (b, c, 0, 0)),
            seq(), seq(),
            const(1, D_RWKV),
            const(1, D_RWKV),
        ],
        out_specs=seq(),
        out_shape=jax.ShapeDtypeStruct((B, L, D_RWKV), F32),
        scratch_shapes=[pltpu.VMEM((Bt, RW_PAIRS, LANES, LANES), F32)],
        compiler_params=pltpu.CompilerParams(
            dimension_semantics=("parallel", "arbitrary"), vmem_limit_bytes=VMEM_LIMIT),
        name="rwkv7",
    )(ah, rh, bh, kh, v, we, bonus, gate, gn_w, gn_b)


def _ffn_kernel(ys_ref, yr_ref, x_ref, wo_ref, gpm_ref, gpre_ref, wup_ref, wdn_ref, gpost_ref,
                o_ref):
    mix = _dot(ys_ref[...], wo_ref[:D_S5, :]) + _dot(yr_ref[0], wo_ref[D_S5:, :])
    h = x_ref[0] + _rms(mix, gpm_ref[...])
    hn = _rms(h, gpre_ref[...])
    up = _dot(hn, wup_ref[...])
    ff = _dot(jnp.square(jnp.maximum(up, 0.0)), wdn_ref[...])
    o_ref[0] = h + _rms(ff, gpost_ref[...])


def _ffn(ys_tm, yr, x, w_out, g_post_mix, g_pre_mlp, w_up, w_down, g_post_mlp):
    B, L, _ = x.shape
    tl = min(ROW_TILE, L)
    const = lambda *shape: pl.BlockSpec(shape, lambda b, i: (0,) * len(shape),
                                        pipeline_mode=pl.Buffered(1))
    return pl.pallas_call(
        _ffn_kernel,
        grid=(B, L // tl),
        in_specs=[
            pl.BlockSpec((tl, D_S5), lambda b, i: (i, b)),
            pl.BlockSpec((1, tl, D_RWKV), lambda b, i: (b, i, 0)),
            pl.BlockSpec((1, tl, D_MODEL), lambda b, i: (b, i, 0)),
            const(D_MODEL, D_MODEL),
            const(1, D_MODEL),
            const(1, D_MODEL),
            const(D_MODEL, D_FF),
            const(D_FF, D_MODEL),
            const(1, D_MODEL),
        ],
        out_specs=pl.BlockSpec((1, tl, D_MODEL), lambda b, i: (b, i, 0)),
        out_shape=jax.ShapeDtypeStruct((B, L, D_MODEL), F32),
        compiler_params=pltpu.CompilerParams(
            dimension_semantics=("parallel", "parallel"), vmem_limit_bytes=VMEM_LIMIT),
        name="ffn",
    )(ys_tm.reshape(L, B * D_S5), yr, x, w_out, g_post_mix, g_pre_mlp, w_up, w_down, g_post_mlp)


def _layer(x, g_pre_mix, w_in, s5_lam_re, s5_lam_im, s5_log_dt, s5_b_re, s5_b_im, s5_c_re, s5_c_im,
           s5_d, s5_w_glu, s5_b_glu, rw_mu, rw_w0, rw_w_up, rw_a0, rw_a_up, rw_g_up, rw_k_k, rw_k_a,
           rw_r_k, rw_gn_w, rw_gn_b, w_out, g_post_mix, g_pre_mlp, w_ff_up, w_ff_down, g_post_mlp):
    row = lambda t: t.reshape(1, -1)
    zeros = jnp.zeros((W_RANK, D_RWKV), F32)
    wwa = jnp.concatenate([jnp.concatenate([rw_w_up, zeros], axis=1),
                           jnp.concatenate([zeros, rw_a_up], axis=1)], axis=0).astype(BF16)
    u_tm, *rw_ops = _inproj(x, row(g_pre_mix), w_in.astype(BF16), row(rw_mu), row(rw_w0), wwa,
                            row(rw_a0), rw_g_up.astype(BF16), row(rw_k_k), row(rw_k_a), row(rw_r_k))
    a_re, a_im, b4, c2 = _s5_params(s5_lam_re, s5_lam_im, s5_log_dt, s5_b_re, s5_b_im,
                                    s5_c_re, s5_c_im)
    ys_tm = _s5(u_tm, a_re, a_im, b4, c2, row(s5_d), s5_w_glu.astype(BF16), row(s5_b_glu))
    yr = _rwkv(*rw_ops, row(rw_gn_w), row(rw_gn_b))
    return _ffn(ys_tm, yr, x, w_out.astype(BF16), row(g_post_mix), row(g_pre_mlp),
                w_ff_up.astype(BF16), w_ff_down.astype(BF16), row(g_post_mlp))


def kernel(x, g_pre_mix, w_in, s5_lam_re, s5_lam_im, s5_log_dt, s5_b_re, s5_b_im, s5_c_re, s5_c_im,
           s5_d, s5_w_glu, s5_b_glu, rw_mu, rw_w0, rw_w_up, rw_a0, rw_a_up, rw_g_up, rw_k_k, rw_k_a,
           rw_r_k, rw_gn_w, rw_gn_b, w_out, g_post_mix, g_pre_mlp, w_ff_up, w_ff_down, g_post_mlp):
    params = (g_pre_mix, w_in, s5_lam_re, s5_lam_im, s5_log_dt, s5_b_re, s5_b_im, s5_c_re, s5_c_im,
              s5_d, s5_w_glu, s5_b_glu, rw_mu, rw_w0, rw_w_up, rw_a0, rw_a_up, rw_g_up, rw_k_k,
              rw_k_a, rw_r_k, rw_gn_w, rw_gn_b, w_out, g_post_mix, g_pre_mlp, w_ff_up, w_ff_down,
              g_post_mlp)
    h = x
    for layer in range(g_pre_mix.shape[0]):
        h = _layer(h, *[p[layer] for p in params])
    return h
```

```python
import functools
import math

import jax
import jax.numpy as jnp
from jax import lax
from jax.experimental import pallas as pl
from jax.experimental.pallas import tpu as pltpu

F32 = jnp.float32
BF16 = jnp.bfloat16

D_MODEL = 1024
D_S5 = 512
D_RWKV = 512
S5_GROUP = 16
S5_GROUPS = 32
S5_STATE = 64
RWKV_HEAD = 64
RWKV_HEADS = 8
W_RANK = 64
A_RANK = 64
G_RANK = 128
N_RW_COLS = 3 * D_RWKV + W_RANK + A_RANK + G_RANK
N_PROJ = D_S5 + N_RW_COLS
D_FF = 4 * D_MODEL
RMS_EPS = 1e-6
GN_EPS = 64e-5

SUBLANES = 8
LANES = 128

ROW_TILE = 512
IN_BATCH = 16
S5_BATCH = SUBLANES
S5_T = 64
S5_GROUPS_PER_STEP = 2
S5_BLOCK_GROUPS = LANES // S5_GROUP
S5_BLOCKS = S5_GROUPS // S5_BLOCK_GROUPS
S5_BLOCK_STATE = S5_BLOCK_GROUPS * S5_STATE
S5_BLOCK_COLS = 2 * S5_BLOCK_STATE
RW_CHUNK = 64
RW_PAIRS = RWKV_HEADS // 2
RW_BT = 8
VMEM_LIMIT = 56 * 1024 * 1024


def _rms(x, g):
    return x * lax.rsqrt(jnp.mean(x * x, axis=-1, keepdims=True) + RMS_EPS) * g


def _dot(a, b):
    return jnp.dot(a.astype(BF16), b.astype(BF16), preferred_element_type=F32)


def _pair_ones():
    r = lax.broadcasted_iota(jnp.int32, (LANES, LANES), 0)
    c = lax.broadcasted_iota(jnp.int32, (LANES, LANES), 1)
    return ((r < RWKV_HEAD) == (c < RWKV_HEAD)).astype(BF16)


def _headsum(x, ones_blk):
    n = x.shape[0]
    tall = jnp.concatenate([x[:, p * LANES:(p + 1) * LANES] for p in range(RW_PAIRS)], axis=0)
    t = jnp.dot(tall.astype(BF16), ones_blk, preferred_element_type=F32)
    return jnp.concatenate([t[p * n:(p + 1) * n] for p in range(RW_PAIRS)], axis=-1)


def _inproj_kernel(x_ref, g_ref, w_ref, mu_ref, w0_ref, wwa_ref, a0_ref, gup_ref, kk_ref, ka_ref,
                   rk_ref, u_ref, ah_ref, rh_ref, bh_ref, kh_ref, v_ref, we_ref, bonus_ref, gate_ref,
                   prev_scr, *, nb, T):
    @pl.when(pl.program_id(1) == 0)
    def _():
        prev_scr[...] = jnp.zeros_like(prev_scr)

    rows = nb * T
    xn = _rms(x_ref[...].reshape(rows, D_MODEL), g_ref[...]).astype(BF16)
    first = lax.broadcasted_iota(jnp.int32, (SUBLANES, LANES), 0) == 0

    def project(c0, c1):
        q = jnp.dot(xn, w_ref[:, D_S5 + c0:D_S5 + c1], preferred_element_type=F32)
        rolled = pltpu.roll(q, 1, 0)
        edge = jnp.concatenate([first] * ((c1 - c0) // LANES), axis=1)
        pieces = []
        for b in range(nb):
            pieces.append(jnp.where(edge, prev_scr[b, :, c0:c1], rolled[b * T:b * T + SUBLANES]))
            pieces.append(rolled[b * T + SUBLANES:(b + 1) * T])
            prev_scr[b, :, c0:c1] = q[(b + 1) * T - 1:(b + 1) * T, :]
        prev = jnp.concatenate(pieces, axis=0)
        return q + mu_ref[:, c0:c1] * (prev - q)

    ones_blk = _pair_ones()
    xw_xa_xg = project(3 * D_RWKV, N_RW_COLS)
    xwa = xw_xa_xg[:, :LANES]
    k = project(D_RWKV, 2 * D_RWKV)
    head0r = lax.broadcasted_iota(jnp.int32, (rows, LANES), 1) < RWKV_HEAD
    wa = _dot(jnp.where(head0r, jnp.tanh(xwa), xwa), wwa_ref[...])
    gate = _dot(jax.nn.sigmoid(xw_xa_xg[:, LANES:]), gup_ref[...])
    r = project(0, D_RWKV)
    kk = k * kk_ref[...]
    kk = kk / jnp.maximum(jnp.sqrt(_headsum(kk * kk, ones_blk)), 1e-12)
    v = project(2 * D_RWKV, 3 * D_RWKV)
    lw = -math.exp(-0.5) * jax.nn.sigmoid(w0_ref[...] + wa[:, :D_RWKV])
    a = jax.nn.sigmoid(a0_ref[...] + wa[:, D_RWKV:])
    k2 = k * (1.0 + (a - 1.0) * ka_ref[...])
    tril = (lax.broadcasted_iota(jnp.int32, (T, T), 0)
            >= lax.broadcasted_iota(jnp.int32, (T, T), 1)).astype(BF16)
    lw_hi = lw.astype(BF16)
    lw_lo = (lw - lw_hi.astype(F32)).astype(BF16)
    cum = jnp.concatenate(
        [jnp.dot(tril, lw_hi[b * T:(b + 1) * T], preferred_element_type=F32)
         + jnp.dot(tril, lw_lo[b * T:(b + 1) * T], preferred_element_type=F32) for b in range(nb)],
        axis=0)
    u = jnp.dot(xn, w_ref[:, :D_S5], preferred_element_type=F32)
    for b in range(nb):
        u_ref[:, b, :] = u[b * T:(b + 1) * T, :]
    bonus = _headsum(r * k2 * rk_ref[...], ones_blk) * v
    w_in = jnp.exp(cum)
    w_inv = jnp.exp(-cum)
    shape = (nb, T, D_RWKV)
    ah_ref[...] = (-kk * jnp.exp(cum - lw)).astype(BF16).reshape(shape)
    rh_ref[...] = (r * w_in).astype(BF16).reshape(shape)
    bh_ref[...] = (kk * a * w_inv).astype(BF16).reshape(shape)
    kh_ref[...] = (k2 * w_inv).astype(BF16).reshape(shape)
    v_ref[...] = v.astype(BF16).reshape(shape)
    bonus_ref[...] = bonus.reshape(shape)
    gate_ref[...] = gate.reshape(shape)
    for b in range(nb):
        we_ref[b, 0] = w_in[(b + 1) * T - 1:(b + 1) * T, :]


def _inproj(x, g, w_in, mu, w0, wwa, a0, g_up, k_k, k_a, r_k):
    B, L, _ = x.shape
    nb = min(IN_BATCH, B)
    T = RW_CHUNK
    const = lambda *shape: pl.BlockSpec(shape, lambda b, i: (0,) * len(shape))
    seq = lambda: pl.BlockSpec((nb, T, D_RWKV), lambda b, i: (b, i, 0))
    bf = jax.ShapeDtypeStruct((B, L, D_RWKV), BF16)
    f32 = jax.ShapeDtypeStruct((B, L, D_RWKV), F32)
    return pl.pallas_call(
        functools.partial(_inproj_kernel, nb=nb, T=T),
        grid=(B // nb, L // T),
        in_specs=[
            pl.BlockSpec((nb, T, D_MODEL), lambda b, i: (b, i, 0)),
            const(1, D_MODEL),
            const(D_MODEL, N_PROJ),
            const(1, N_RW_COLS),
            const(1, D_RWKV),
            const(LANES, 2 * D_RWKV),
            const(1, D_RWKV),
            const(G_RANK, D_RWKV),
            const(1, D_RWKV),
            const(1, D_RWKV),
            const(1, D_RWKV),
        ],
        out_specs=[
            pl.BlockSpec((T, nb, D_S5), lambda b, i: (i, b, 0)),
            seq(), seq(), seq(), seq(), seq(),
            pl.BlockSpec((nb, 1, 1, D_RWKV), lambda b, i: (b, i, 0, 0)),
            seq(), seq(),
        ],
        out_shape=[
            jax.ShapeDtypeStruct((L, B, D_S5), F32),
            bf, bf, bf, bf, bf,
            jax.ShapeDtypeStruct((B, L // T, 1, D_RWKV), F32),
            f32, f32,
        ],
        scratch_shapes=[pltpu.VMEM((nb, 1, N_RW_COLS), F32)],
        compiler_params=pltpu.CompilerParams(
            dimension_semantics=("parallel", "arbitrary"), vmem_limit_bytes=VMEM_LIMIT),
        name="inproj",
    )(x, g, w_in, mu, w0, wwa, a0, g_up, k_k, k_a, r_k)


def _s5_kernel(u_ref, are_ref, aim_ref, b4_ref, c2_ref, d_ref, wglu_ref, bglu_ref,
               y_ref, s_scr, bu_scr, sb_scr, *, T, NG):
    rows = T * S5_BATCH
    pair_cols = 2 * S5_BLOCK_COLS

    @pl.when(pl.program_id(1) == 0)
    def _():
        s_scr[...] = jnp.zeros_like(s_scr)

    us = [u_ref[:, g * S5_BATCH:(g + 1) * S5_BATCH, :].reshape(rows, D_S5) for g in range(NG)]
    ubs = [u.astype(BF16) for u in us]

    def bu(g, h):
        for q in (2 * h, 2 * h + 1):
            bu_scr[g, :, q * S5_BLOCK_COLS:(q + 1) * S5_BLOCK_COLS] = jnp.dot(
                ubs[g][:, q * LANES:(q + 1) * LANES], b4_ref[q], preferred_element_type=F32)

    def scan(g, h):
        qs = (2 * h, 2 * h + 1)
        a_bc = {}
        st = {}
        for q in qs:
            a0 = q * S5_BLOCK_STATE
            a_bc[q] = (
                jnp.broadcast_to(are_ref[:, a0:a0 + S5_BLOCK_STATE], (S5_BATCH, S5_BLOCK_STATE)),
                jnp.broadcast_to(aim_ref[:, a0:a0 + S5_BLOCK_STATE], (S5_BATCH, S5_BLOCK_STATE)))
            re0 = q * S5_BLOCK_COLS
            st[q] = (s_scr[g, :, re0:re0 + S5_BLOCK_STATE],
                     s_scr[g, :, re0 + S5_BLOCK_STATE:re0 + S5_BLOCK_COLS])
        held = {}
        for t in range(T):
            r0 = t * S5_BATCH
            for q in qs:
                re0 = q * S5_BLOCK_COLS
                im0 = re0 + S5_BLOCK_STATE
                are, aim = a_bc[q]
                sre, sim = st[q]
                nre = are * sre - aim * sim + bu_scr[g, r0:r0 + S5_BATCH, re0:re0 + S5_BLOCK_STATE]
                nim = are * sim + aim * sre + bu_scr[g, r0:r0 + S5_BATCH, im0:im0 + S5_BLOCK_STATE]
                st[q] = (nre, nim)
                if t % 2 == 0:
                    held[q] = (nre, nim)
                else:
                    p0 = r0 - S5_BATCH
                    sb_scr[g, p0:p0 + 2 * S5_BATCH, re0:re0 + S5_BLOCK_STATE] = jnp.concatenate(
                        [held[q][0], nre], axis=0).astype(BF16)
                    sb_scr[g, p0:p0 + 2 * S5_BATCH, im0:im0 + S5_BLOCK_STATE] = jnp.concatenate(
                        [held[q][1], nim], axis=0).astype(BF16)
        for q in qs:
            re0 = q * S5_BLOCK_COLS
            s_scr[g, :, re0:re0 + S5_BLOCK_STATE] = st[q][0]
            s_scr[g, :, re0 + S5_BLOCK_STATE:re0 + S5_BLOCK_COLS] = st[q][1]

    def cs(g, h):
        return jnp.dot(sb_scr[g, :, h * pair_cols:(h + 1) * pair_cols], c2_ref[h],
                       preferred_element_type=F32)

    def gelu_in(g, ys):
        return jax.nn.gelu(jnp.concatenate(ys, axis=-1) + d_ref[...] * us[g])

    def glu_out(g, z):
        gate = jax.nn.sigmoid(_dot(z, wglu_ref[...]) + bglu_ref[...])
        y_ref[:, g * S5_BATCH:(g + 1) * S5_BATCH, :] = (z * gate).reshape(T, S5_BATCH, D_S5)

    n_pairs = S5_BLOCKS // 2
    for h in range(n_pairs):
        bu(0, h)
    ys = {g: [] for g in range(NG)}
    for g in range(NG):
        for h in range(n_pairs):
            scan(g, h)
            if g + 1 < NG:
                bu(g + 1, h)
            ys[g].append(cs(g, h))
            if g > 0 and h == 0:
                glu_out(g - 1, gelu_in(g - 1, ys[g - 1]))
    glu_out(NG - 1, gelu_in(NG - 1, ys[NG - 1]))


def _s5_params(lam_re, lam_im, log_dt, b_re, b_im, c_re, c_im):
    dt = jnp.exp(log_dt)[:, None]
    mag = jnp.exp(lam_re * dt)
    ab_re = mag * jnp.cos(lam_im * dt)
    ab_im = mag * jnp.sin(lam_im * dt)
    den = lam_re * lam_re + lam_im * lam_im
    nr = ab_re - 1.0
    f_re = (nr * lam_re + ab_im * lam_im) / den
    f_im = (ab_im * lam_re - nr * lam_im) / den
    bb_re = f_re[..., None] * b_re - f_im[..., None] * b_im
    bb_im = f_re[..., None] * b_im + f_im[..., None] * b_re
    eye = jnp.eye(S5_BLOCK_GROUPS, dtype=F32)

    def blk(t, q):
        return t[q * S5_BLOCK_GROUPS:(q + 1) * S5_BLOCK_GROUPS]

    def b_block(bb, q):
        return jnp.einsum("gpc,gk->gckp", blk(bb, q), eye).reshape(LANES, S5_BLOCK_STATE)

    def c_block(cc, q):
        return jnp.einsum("gcp,gk->gpkc", blk(cc, q), eye).reshape(S5_BLOCK_STATE, LANES)

    b4 = jnp.stack([jnp.concatenate([b_block(bb_re, q), b_block(bb_im, q)], axis=1)
                    for q in range(S5_BLOCKS)]).astype(BF16)
    c4 = [jnp.concatenate([c_block(c_re, q), -c_block(c_im, q)], axis=0)
          for q in range(S5_BLOCKS)]
    zc = jnp.zeros_like(c4[0])
    c2 = jnp.stack([jnp.concatenate([jnp.concatenate([c4[2 * h], zc], axis=1),
                                     jnp.concatenate([zc, c4[2 * h + 1]], axis=1)], axis=0)
                    for h in range(S5_BLOCKS // 2)]).astype(BF16)
    return ab_re.reshape(1, -1), ab_im.reshape(1, -1), b4, c2


def _s5(u_tm, a_re, a_im, b4, c2, d_skip, w_glu, b_glu):
    L, B, _ = u_tm.shape
    T = min(S5_T, L)
    NG = min(S5_GROUPS_PER_STEP, B // S5_BATCH)
    n_state = S5_BLOCKS * S5_BLOCK_COLS
    const = lambda *shape: pl.BlockSpec(shape, lambda b, t: (0,) * len(shape))
    return pl.pallas_call(
        functools.partial(_s5_kernel, T=T, NG=NG),
        grid=(B // (NG * S5_BATCH), L // T),
        in_specs=[
            pl.BlockSpec((T, NG * S5_BATCH, D_S5), lambda b, t: (t, b, 0)),
            const(1, S5_GROUPS * S5_STATE),
            const(1, S5_GROUPS * S5_STATE),
            const(S5_BLOCKS, LANES, S5_BLOCK_COLS),
            const(S5_BLOCKS // 2, 2 * S5_BLOCK_COLS, 2 * LANES),
            const(1, D_S5),
            const(D_S5, D_S5),
            const(1, D_S5),
        ],
        out_specs=pl.BlockSpec((T, NG * S5_BATCH, D_S5), lambda b, t: (t, b, 0)),
        out_shape=jax.ShapeDtypeStruct((L, B, D_S5), F32),
        scratch_shapes=[
            pltpu.VMEM((NG, S5_BATCH, n_state), F32),
            pltpu.VMEM((NG, T * S5_BATCH, n_state), F32),
            pltpu.VMEM((NG, T * S5_BATCH, n_state), BF16),
        ],
        compiler_params=pltpu.CompilerParams(
            dimension_semantics=("parallel", "arbitrary"), vmem_limit_bytes=VMEM_LIMIT),
        name="s5",
    )(u_tm, a_re, a_im, b4, c2, d_skip, w_glu, b_glu)


def _rwkv_kernel(ah_ref, rh_ref, bh_ref, kh_ref, v_ref, we_ref, bonus_ref, gate_ref, gnw_ref,
                 gnb_ref, o_ref, s_scr, *, C, Bt):
    @pl.when(pl.program_id(1) == 0)
    def _():
        s_scr[...] = jnp.zeros_like(s_scr)

    row = lax.broadcasted_iota(jnp.int32, (C, LANES), 0)
    lane = lax.broadcasted_iota(jnp.int32, (C, LANES), 1)
    col = lane & (RWKV_HEAD - 1)
    head0 = lane < RWKV_HEAD
    strict = col < row
    incl = col <= row
    eye = (col == row).astype(F32)
    blk = row ^ col
    ones_blk = _pair_ones()
    same_head = ones_blk > 0
    sl = [slice(p * LANES, (p + 1) * LANES) for p in range(RW_PAIRS)]
    bis = list(range(Bt))
    ch = [(bi, p) for bi in bis for p in range(RW_PAIRS)]

    def bd(y):
        zero = jnp.zeros_like(y)
        return jnp.concatenate([jnp.where(head0, y, zero), jnp.where(head0, zero, y)], axis=0)

    def bmm(x, y):
        return _dot(x, bd(y))

    nt = (((1,), (1,)), ((), ()))
    tn = (((0,), (0,)), ((), ()))

    def each(fn, *lists):
        return [fn(*args) for args in zip(*lists)]

    ahp = [ah_ref[bi][:, sl[p]] for bi, p in ch]
    rhp = [rh_ref[bi][:, sl[p]] for bi, p in ch]
    bhp = [bh_ref[bi][:, sl[p]] for bi, p in ch]
    khp = [kh_ref[bi][:, sl[p]] for bi, p in ch]
    vp = [v_ref[bi][:, sl[p]] for bi, p in ch]
    wep = [we_ref[bi, 0][:, sl[p]] for bi, p in ch]
    S = [s_scr[bi, p] for bi, p in ch]

    zero = jnp.zeros((C, LANES), BF16)
    lhs = each(lambda x, y: jnp.concatenate([x, y], axis=0), ahp, rhp)
    lhs_m = each(lambda x, y: jnp.concatenate(
        [jnp.where(head0, x, zero), jnp.where(head0, zero, x),
         jnp.where(head0, y, zero), jnp.where(head0, zero, y)], axis=0), ahp, rhp)
    rhs = each(lambda x, y: jnp.concatenate([x, y], axis=0), bhp, khp)
    gram = each(lambda x, y: lax.dot_general(x, y, nt, preferred_element_type=F32), lhs_m, rhs)
    swap = lambda t: pltpu.roll(t, RWKV_HEAD, 1)
    a_ab = [jnp.where(strict, jnp.where(head0, t[:C], swap(t[C:2 * C])), 0.0) for t in gram]
    a_ak = [jnp.where(strict, jnp.where(head0, swap(t[:C]), t[C:2 * C]), 0.0) for t in gram]
    r_b = [jnp.where(incl, jnp.where(head0, t[2 * C:3 * C], swap(t[3 * C:])), 0.0) for t in gram]
    r_k = [jnp.where(incl, jnp.where(head0, swap(t[2 * C:3 * C]), t[3 * C:]), 0.0) for t in gram]
    ss = each(lambda x, st: lax.dot_general(x, st.astype(BF16), nt, preferred_element_type=F32),
              lhs, S)
    akv = each(lambda aa, rk_, vv: bmm(jnp.concatenate([aa, rk_], axis=0), vv), a_ak, r_k, vp)
    pm = [jnp.where(blk < 8, t, 0.0) for t in a_ab]
    p2 = each(bmm, pm, pm)
    ip = [eye + t for t in pm]
    both = each(lambda x, y: bmm(jnp.concatenate([x, y], axis=0), x), p2, ip)
    t_inv = each(lambda x, y: x + y[C:], ip, both)
    t_inv = each(lambda x, y: x + bmm(x, y[:C]), t_inv, both)
    gsz = 8
    while gsz < C:
        m = [jnp.where((blk >= gsz) & (blk < 2 * gsz), t, 0.0) for t in a_ab]
        tm = each(bmm, t_inv, m)
        t_inv = each(lambda x, y: x + bmm(y, x), t_inv, tm)
        gsz *= 2
    x = each(lambda s_, t: s_[:C] + t[:C], ss, akv)
    u = each(bmm, t_inv, x)
    o = each(lambda s_, t, rb, uu: s_[C:] + t[C:] + bmm(rb, uu), ss, akv, r_b, u)
    upd = each(lambda uu, vv, bb, we: lax.dot_general(
        jnp.concatenate([uu.astype(BF16), vv], axis=0), (bb * we).astype(BF16), tn,
        preferred_element_type=F32), u, vp, rhs, wep)
    for (bi, p), st, we, up in zip(ch, S, wep, upd):
        s_scr[bi, p] = st * we + jnp.where(same_head, up, 0.0)
    o = jnp.concatenate(
        [jnp.concatenate(o[bi * RW_PAIRS:(bi + 1) * RW_PAIRS], axis=-1) for bi in bis], axis=0)
    dlt = o - _headsum(o, ones_blk) * (1.0 / RWKV_HEAD)
    var = _headsum(dlt * dlt, ones_blk) * (1.0 / RWKV_HEAD)
    on = dlt * lax.rsqrt(var + GN_EPS) * gnw_ref[...] + gnb_ref[...]
    rows = Bt * C
    out = (on + bonus_ref[...].reshape(rows, D_RWKV)) * gate_ref[...].reshape(rows, D_RWKV)
    o_ref[...] = out.reshape(Bt, C, D_RWKV)


def _rwkv(ah, rh, bh, kh, v, we, bonus, gate, gn_w, gn_b, *, Bt=RW_BT):
    B, L, _ = ah.shape
    C = RW_CHUNK
    Bt = min(Bt, B)
    seq = lambda: pl.BlockSpec((Bt, C, D_RWKV), lambda b, c: (b, c, 0))
    const = lambda *shape: pl.BlockSpec(shape, lambda b, c: (0,) * len(shape))
    return pl.pallas_call(
        functools.partial(_rwkv_kernel, C=C, Bt=Bt),
        grid=(B // Bt, L // C),
        in_specs=[
            seq(), seq(), seq(), seq(), seq(),
            pl.BlockSpec((Bt, 1, 1, D_RWKV), lambda b, c: (b, c, 0, 0)),
            seq(), seq(),
            const(1, D_RWKV),
            const(1, D_RWKV),
        ],
        out_specs=seq(),
        out_shape=jax.ShapeDtypeStruct((B, L, D_RWKV), F32),
        scratch_shapes=[pltpu.VMEM((Bt, RW_PAIRS, LANES, LANES), F32)],
        compiler_params=pltpu.CompilerParams(
            dimension_semantics=("parallel", "arbitrary"), vmem_limit_bytes=VMEM_LIMIT),
        name="rwkv7",
    )(ah, rh, bh, kh, v, we, bonus, gate, gn_w, gn_b)


def _ffn_kernel(ys_ref, yr_ref, x_ref, wo_ref, gpm_ref, gpre_ref, wup_ref, wdn_ref, gpost_ref,
                o_ref):
    n = ys_ref.shape[0] // 2
    halves = [slice(0, n), slice(n, 2 * n)]
    mix = [_dot(ys_ref[s, :], wo_ref[:D_S5, :]) + _dot(yr_ref[0, s, :], wo_ref[D_S5:, :])
           for s in halves]
    h = [x_ref[0, s, :] + _rms(m, gpm_ref[...]) for s, m in zip(halves, mix)]
    up = [_dot(_rms(t, gpre_ref[...]), wup_ref[...]) for t in h]
    ff = [_dot(jnp.square(jnp.maximum(t, 0.0)), wdn_ref[...]) for t in up]
    for s, t, f in zip(halves, h, ff):
        o_ref[0, s, :] = t + _rms(f, gpost_ref[...])


def _ffn(ys_tm, yr, x, w_out, g_post_mix, g_pre_mlp, w_up, w_down, g_post_mlp):
    B, L, _ = x.shape
    tl = min(ROW_TILE, L)
    const = lambda *shape: pl.BlockSpec(shape, lambda b, i: (0,) * len(shape),
                                        pipeline_mode=pl.Buffered(1))
    return pl.pallas_call(
        _ffn_kernel,
        grid=(B, L // tl),
        in_specs=[
            pl.BlockSpec((tl, D_S5), lambda b, i: (i, b)),
            pl.BlockSpec((1, tl, D_RWKV), lambda b, i: (b, i, 0)),
            pl.BlockSpec((1, tl, D_MODEL), lambda b, i: (b, i, 0)),
            const(D_MODEL, D_MODEL),
            const(1, D_MODEL),
            const(1, D_MODEL),
            const(D_MODEL, D_FF),
            const(D_FF, D_MODEL),
            const(1, D_MODEL),
        ],
        out_specs=pl.BlockSpec((1, tl, D_MODEL), lambda b, i: (b, i, 0)),
        out_shape=jax.ShapeDtypeStruct((B, L, D_MODEL), F32),
        compiler_params=pltpu.CompilerParams(
            dimension_semantics=("parallel", "parallel"), vmem_limit_bytes=VMEM_LIMIT),
        name="ffn",
    )(ys_tm.reshape(L, B * D_S5), yr, x, w_out, g_post_mix, g_pre_mlp, w_up, w_down, g_post_mlp)


def _layer(x, g_pre_mix, w_in, s5_lam_re, s5_lam_im, s5_log_dt, s5_b_re, s5_b_im, s5_c_re, s5_c_im,
           s5_d, s5_w_glu, s5_b_glu, rw_mu, rw_w0, rw_w_up, rw_a0, rw_a_up, rw_g_up, rw_k_k, rw_k_a,
           rw_r_k, rw_gn_w, rw_gn_b, w_out, g_post_mix, g_pre_mlp, w_ff_up, w_ff_down, g_post_mlp):
    row = lambda t: t.reshape(1, -1)
    zeros = jnp.zeros((W_RANK, D_RWKV), F32)
    wwa = jnp.concatenate([jnp.concatenate([rw_w_up, zeros], axis=1),
                           jnp.concatenate([zeros, rw_a_up], axis=1)], axis=0).astype(BF16)
    u_tm, *rw_ops = _inproj(x, row(g_pre_mix), w_in.astype(BF16), row(rw_mu), row(rw_w0), wwa,
                            row(rw_a0), rw_g_up.astype(BF16), row(rw_k_k), row(rw_k_a), row(rw_r_k))
    a_re, a_im, b4, c2 = _s5_params(s5_lam_re, s5_lam_im, s5_log_dt, s5_b_re, s5_b_im,
                                    s5_c_re, s5_c_im)
    ys_tm = _s5(u_tm, a_re, a_im, b4, c2, row(s5_d), s5_w_glu.astype(BF16), row(s5_b_glu))
    yr = _rwkv(*rw_ops, row(rw_gn_w), row(rw_gn_b))
    return _ffn(ys_tm, yr, x, w_out.astype(BF16), row(g_post_mix), row(g_pre_mlp),
                w_ff_up.astype(BF16), w_ff_down.astype(BF16), row(g_post_mlp))


def kernel(x, g_pre_mix, w_in, s5_lam_re, s5_lam_im, s5_log_dt, s5_b_re, s5_b_im, s5_c_re, s5_c_im,
           s5_d, s5_w_glu, s5_b_glu, rw_mu, rw_w0, rw_w_up, rw_a0, rw_a_up, rw_g_up, rw_k_k, rw_k_a,
           rw_r_k, rw_gn_w, rw_gn_b, w_out, g_post_mix, g_pre_mlp, w_ff_up, w_ff_down, g_post_mlp):
    params = (g_pre_mix, w_in, s5_lam_re, s5_lam_im, s5_log_dt, s5_b_re, s5_b_im, s5_c_re, s5_c_im,
              s5_d, s5_w_glu, s5_b_glu, rw_mu, rw_w0, rw_w_up, rw_a0, rw_a_up, rw_g_up, rw_k_k,
              rw_k_a, rw_r_k, rw_gn_w, rw_gn_b, w_out, g_post_mix, g_pre_mlp, w_ff_up, w_ff_down,
              g_post_mlp)
    h = x
    for layer in range(g_pre_mix.shape[0]):
        h = _layer(h, *[p[layer] for p in params])
    return h
```

```python
import functools
import math

import jax
import jax.numpy as jnp
from jax import lax
from jax.experimental import pallas as pl
from jax.experimental.pallas import tpu as pltpu

F32 = jnp.float32
BF16 = jnp.bfloat16

D_MODEL = 1024
D_S5 = 512
D_RWKV = 512
S5_GROUP = 16
S5_GROUPS = 32
S5_STATE = 64
RWKV_HEAD = 64
RWKV_HEADS = 8
W_RANK = 64
A_RANK = 64
G_RANK = 128
N_RW_COLS = 3 * D_RWKV + W_RANK + A_RANK + G_RANK
N_PROJ = D_S5 + N_RW_COLS
D_FF = 4 * D_MODEL
RMS_EPS = 1e-6
GN_EPS = 64e-5

SUBLANES = 8
LANES = 128

ROW_TILE = 512
IN_BATCH = 16
S5_BATCH = SUBLANES
S5_T = 64
S5_GROUPS_PER_STEP = 2
S5_BLOCK_GROUPS = LANES // S5_GROUP
S5_BLOCKS = S5_GROUPS // S5_BLOCK_GROUPS
S5_BLOCK_STATE = S5_BLOCK_GROUPS * S5_STATE
S5_BLOCK_COLS = 2 * S5_BLOCK_STATE
RW_CHUNK = 64
RW_PAIRS = RWKV_HEADS // 2
RW_BT = 8
VMEM_LIMIT = 56 * 1024 * 1024


def _rms(x, g):
    return x * lax.rsqrt(jnp.mean(x * x, axis=-1, keepdims=True) + RMS_EPS) * g


def _dot(a, b):
    return jnp.dot(a.astype(BF16), b.astype(BF16), preferred_element_type=F32)


def _pair_ones():
    r = lax.broadcasted_iota(jnp.int32, (LANES, LANES), 0)
    c = lax.broadcasted_iota(jnp.int32, (LANES, LANES), 1)
    return ((r < RWKV_HEAD) == (c < RWKV_HEAD)).astype(BF16)


def _headsum(x, ones_blk):
    n = x.shape[0]
    tall = jnp.concatenate([x[:, p * LANES:(p + 1) * LANES] for p in range(RW_PAIRS)], axis=0)
    t = jnp.dot(tall.astype(BF16), ones_blk, preferred_element_type=F32)
    return jnp.concatenate([t[p * n:(p + 1) * n] for p in range(RW_PAIRS)], axis=-1)


def _inproj_kernel(x_ref, g_ref, w_ref, mu_ref, w0_ref, wwa_ref, a0_ref, gup_ref, kk_ref, ka_ref,
                   rk_ref, u_ref, ah_ref, rh_ref, bh_ref, kh_ref, v_ref, we_ref, bonus_ref, gate_ref,
                   prev_scr, *, nb, T):
    @pl.when(pl.program_id(1) == 0)
    def _():
        prev_scr[...] = jnp.zeros_like(prev_scr)

    rows = nb * T
    xn = _rms(x_ref[...].reshape(rows, D_MODEL), g_ref[...]).astype(BF16)
    first = lax.broadcasted_iota(jnp.int32, (SUBLANES, LANES), 0) == 0

    def project(c0, c1):
        q = jnp.dot(xn, w_ref[:, D_S5 + c0:D_S5 + c1], preferred_element_type=F32)
        rolled = pltpu.roll(q, 1, 0)
        edge = jnp.concatenate([first] * ((c1 - c0) // LANES), axis=1)
        pieces = []
        for b in range(nb):
            pieces.append(jnp.where(edge, prev_scr[b, :, c0:c1], rolled[b * T:b * T + SUBLANES]))
            pieces.append(rolled[b * T + SUBLANES:(b + 1) * T])
            prev_scr[b, :, c0:c1] = q[(b + 1) * T - 1:(b + 1) * T, :]
        prev = jnp.concatenate(pieces, axis=0)
        return q + mu_ref[:, c0:c1] * (prev - q)

    ones_blk = _pair_ones()
    xw_xa_xg = project(3 * D_RWKV, N_RW_COLS)
    xwa = xw_xa_xg[:, :LANES]
    k = project(D_RWKV, 2 * D_RWKV)
    head0r = lax.broadcasted_iota(jnp.int32, (rows, LANES), 1) < RWKV_HEAD
    wa = _dot(jnp.where(head0r, jnp.tanh(xwa), xwa), wwa_ref[...])
    gate = _dot(jax.nn.sigmoid(xw_xa_xg[:, LANES:]), gup_ref[...])
    r = project(0, D_RWKV)
    kk = k * kk_ref[...]
    kk = kk / jnp.maximum(jnp.sqrt(_headsum(kk * kk, ones_blk)), 1e-12)
    v = project(2 * D_RWKV, 3 * D_RWKV)
    lw = -math.exp(-0.5) * jax.nn.sigmoid(w0_ref[...] + wa[:, :D_RWKV])
    a = jax.nn.sigmoid(a0_ref[...] + wa[:, D_RWKV:])
    k2 = k * (1.0 + (a - 1.0) * ka_ref[...])
    tril = (lax.broadcasted_iota(jnp.int32, (T, T), 0)
            >= lax.broadcasted_iota(jnp.int32, (T, T), 1)).astype(BF16)
    lw_hi = lw.astype(BF16)
    lw_lo = (lw - lw_hi.astype(F32)).astype(BF16)
    cum = jnp.concatenate(
        [jnp.dot(tril, lw_hi[b * T:(b + 1) * T], preferred_element_type=F32)
         + jnp.dot(tril, lw_lo[b * T:(b + 1) * T], preferred_element_type=F32) for b in range(nb)],
        axis=0)
    u = jnp.dot(xn, w_ref[:, :D_S5], preferred_element_type=F32)
    for b in range(nb):
        u_ref[:, b, :] = u[b * T:(b + 1) * T, :]
    bonus = _headsum(r * k2 * rk_ref[...], ones_blk) * v
    w_in = jnp.exp(cum)
    w_inv = jnp.exp(-cum)
    shape = (nb, T, D_RWKV)
    ah_ref[...] = (-kk * jnp.exp(cum - lw)).astype(BF16).reshape(shape)
    rh_ref[...] = (r * w_in).astype(BF16).reshape(shape)
    bh_ref[...] = (kk * a * w_inv).astype(BF16).reshape(shape)
    kh_ref[...] = (k2 * w_inv).astype(BF16).reshape(shape)
    v_ref[...] = v.astype(BF16).reshape(shape)
    bonus_ref[...] = bonus.reshape(shape)
    gate_ref[...] = gate.reshape(shape)
    for b in range(nb):
        we_ref[b, 0] = w_in[(b + 1) * T - 1:(b + 1) * T, :]


def _inproj(x, g, w_in, mu, w0, wwa, a0, g_up, k_k, k_a, r_k):
    B, L, _ = x.shape
    nb = min(IN_BATCH, B)
    T = RW_CHUNK
    const = lambda *shape: pl.BlockSpec(shape, lambda b, i: (0,) * len(shape))
    seq = lambda: pl.BlockSpec((nb, T, D_RWKV), lambda b, i: (b, i, 0))
    bf = jax.ShapeDtypeStruct((B, L, D_RWKV), BF16)
    f32 = jax.ShapeDtypeStruct((B, L, D_RWKV), F32)
    return pl.pallas_call(
        functools.partial(_inproj_kernel, nb=nb, T=T),
        grid=(B // nb, L // T),
        in_specs=[
            pl.BlockSpec((nb, T, D_MODEL), lambda b, i: (b, i, 0)),
            const(1, D_MODEL),
            const(D_MODEL, N_PROJ),
            const(1, N_RW_COLS),
            const(1, D_RWKV),
            const(LANES, 2 * D_RWKV),
            const(1, D_RWKV),
            const(G_RANK, D_RWKV),
            const(1, D_RWKV),
            const(1, D_RWKV),
            const(1, D_RWKV),
        ],
        out_specs=[
            pl.BlockSpec((T, nb, D_S5), lambda b, i: (i, b, 0)),
            seq(), seq(), seq(), seq(), seq(),
            pl.BlockSpec((nb, 1, 1, D_RWKV), lambda b, i: (b, i, 0, 0)),
            seq(), seq(),
        ],
        out_shape=[
            jax.ShapeDtypeStruct((L, B, D_S5), F32),
            bf, bf, bf, bf, bf,
            jax.ShapeDtypeStruct((B, L // T, 1, D_RWKV), F32),
            f32, f32,
        ],
        scratch_shapes=[pltpu.VMEM((nb, 1, N_RW_COLS), F32)],
        compiler_params=pltpu.CompilerParams(
            dimension_semantics=("parallel", "arbitrary"), vmem_limit_bytes=VMEM_LIMIT),
        name="inproj",
    )(x, g, w_in, mu, w0, wwa, a0, g_up, k_k, k_a, r_k)


def _s5_kernel(u_ref, are_ref, aim_ref, b4_ref, c2_ref, d_ref, wglu_ref, bglu_ref,
               y_ref, s_scr, bu_scr, sb_scr, *, T, NG):
    rows = T * S5_BATCH
    pair_cols = 2 * S5_BLOCK_COLS

    @pl.when(pl.program_id(1) == 0)
    def _():
        s_scr[...] = jnp.zeros_like(s_scr)

    us = [u_ref[:, g * S5_BATCH:(g + 1) * S5_BATCH, :].reshape(rows, D_S5) for g in range(NG)]
    ubs = [u.astype(BF16) for u in us]

    def bu(g, h):
        for q in (2 * h, 2 * h + 1):
            bu_scr[g, :, q * S5_BLOCK_COLS:(q + 1) * S5_BLOCK_COLS] = jnp.dot(
                ubs[g][:, q * LANES:(q + 1) * LANES], b4_ref[q], preferred_element_type=F32)

    def scan(g, h):
        qs = (2 * h, 2 * h + 1)
        a_bc = {}
        st = {}
        for q in qs:
            a0 = q * S5_BLOCK_STATE
            a_bc[q] = (
                jnp.broadcast_to(are_ref[:, a0:a0 + S5_BLOCK_STATE], (S5_BATCH, S5_BLOCK_STATE)),
                jnp.broadcast_to(aim_ref[:, a0:a0 + S5_BLOCK_STATE], (S5_BATCH, S5_BLOCK_STATE)))
            re0 = q * S5_BLOCK_COLS
            st[q] = (s_scr[g, :, re0:re0 + S5_BLOCK_STATE],
                     s_scr[g, :, re0 + S5_BLOCK_STATE:re0 + S5_BLOCK_COLS])
        held = {}
        for t in range(T):
            r0 = t * S5_BATCH
            for q in qs:
                re0 = q * S5_BLOCK_COLS
                im0 = re0 + S5_BLOCK_STATE
                are, aim = a_bc[q]
                sre, sim = st[q]
                nre = are * sre - aim * sim + bu_scr[g, r0:r0 + S5_BATCH, re0:re0 + S5_BLOCK_STATE]
                nim = are * sim + aim * sre + bu_scr[g, r0:r0 + S5_BATCH, im0:im0 + S5_BLOCK_STATE]
                st[q] = (nre, nim)
                if t % 2 == 0:
                    held[q] = (nre, nim)
                else:
                    p0 = r0 - S5_BATCH
                    sb_scr[g, p0:p0 + 2 * S5_BATCH, re0:re0 + S5_BLOCK_STATE] = jnp.concatenate(
                        [held[q][0], nre], axis=0).astype(BF16)
                    sb_scr[g, p0:p0 + 2 * S5_BATCH, im0:im0 + S5_BLOCK_STATE] = jnp.concatenate(
                        [held[q][1], nim], axis=0).astype(BF16)
        for q in qs:
            re0 = q * S5_BLOCK_COLS
            s_scr[g, :, re0:re0 + S5_BLOCK_STATE] = st[q][0]
            s_scr[g, :, re0 + S5_BLOCK_STATE:re0 + S5_BLOCK_COLS] = st[q][1]

    def cs(g, h):
        return jnp.dot(sb_scr[g, :, h * pair_cols:(h + 1) * pair_cols], c2_ref[h],
                       preferred_element_type=F32)

    def gelu_in(g, ys):
        return jax.nn.gelu(jnp.concatenate(ys, axis=-1) + d_ref[...] * us[g])

    def glu_out(g, z):
        gate = jax.nn.sigmoid(_dot(z, wglu_ref[...]) + bglu_ref[...])
        y_ref[:, g * S5_BATCH:(g + 1) * S5_BATCH, :] = (z * gate).reshape(T, S5_BATCH, D_S5)

    n_pairs = S5_BLOCKS // 2
    for h in range(n_pairs):
        bu(0, h)
    ys = {g: [] for g in range(NG)}
    for g in range(NG):
        for h in range(n_pairs):
            scan(g, h)
            if g + 1 < NG:
                bu(g + 1, h)
            ys[g].append(cs(g, h))
            if g > 0 and h == 0:
                glu_out(g - 1, gelu_in(g - 1, ys[g - 1]))
    glu_out(NG - 1, gelu_in(NG - 1, ys[NG - 1]))


def _s5_params(lam_re, lam_im, log_dt, b_re, b_im, c_re, c_im):
    dt = jnp.exp(log_dt)[:, None]
    mag = jnp.exp(lam_re * dt)
    ab_re = mag * jnp.cos(lam_im * dt)
    ab_im = mag * jnp.sin(lam_im * dt)
    den = lam_re * lam_re + lam_im * lam_im
    nr = ab_re - 1.0
    f_re = (nr * lam_re + ab_im * lam_im) / den
    f_im = (ab_im * lam_re - nr * lam_im) / den
    bb_re = f_re[..., None] * b_re - f_im[..., None] * b_im
    bb_im = f_re[..., None] * b_im + f_im[..., None] * b_re
    eye = jnp.eye(S5_BLOCK_GROUPS, dtype=F32)

    def blk(t, q):
        return t[q * S5_BLOCK_GROUPS:(q + 1) * S5_BLOCK_GROUPS]

    def b_block(bb, q):
        return jnp.einsum("gpc,gk->gckp", blk(bb, q), eye).reshape(LANES, S5_BLOCK_STATE)

    def c_block(cc, q):
        return jnp.einsum("gcp,gk->gpkc", blk(cc, q), eye).reshape(S5_BLOCK_STATE, LANES)

    b4 = jnp.stack([jnp.concatenate([b_block(bb_re, q), b_block(bb_im, q)], axis=1)
                    for q in range(S5_BLOCKS)]).astype(BF16)
    c4 = [jnp.concatenate([c_block(c_re, q), -c_block(c_im, q)], axis=0)
          for q in range(S5_BLOCKS)]
    zc = jnp.zeros_like(c4[0])
    c2 = jnp.stack([jnp.concatenate([jnp.concatenate([c4[2 * h], zc], axis=1),
                                     jnp.concatenate([zc, c4[2 * h + 1]], axis=1)], axis=0)
                    for h in range(S5_BLOCKS // 2)]).astype(BF16)
    return ab_re.reshape(1, -1), ab_im.reshape(1, -1), b4, c2


def _s5(u_tm, a_re, a_im, b4, c2, d_skip, w_glu, b_glu):
    L, B, _ = u_tm.shape
    T = min(S5_T, L)
    NG = min(S5_GROUPS_PER_STEP, B // S5_BATCH)
    n_state = S5_BLOCKS * S5_BLOCK_COLS
    const = lambda *shape: pl.BlockSpec(shape, lambda b, t: (0,) * len(shape))
    return pl.pallas_call(
        functools.partial(_s5_kernel, T=T, NG=NG),
        grid=(B // (NG * S5_BATCH), L // T),
        in_specs=[
            pl.BlockSpec((T, NG * S5_BATCH, D_S5), lambda b, t: (t, b, 0)),
            const(1, S5_GROUPS * S5_STATE),
            const(1, S5_GROUPS * S5_STATE),
            const(S5_BLOCKS, LANES, S5_BLOCK_COLS),
            const(S5_BLOCKS // 2, 2 * S5_BLOCK_COLS, 2 * LANES),
            const(1, D_S5),
            const(D_S5, D_S5),
            const(1, D_S5),
        ],
        out_specs=pl.BlockSpec((T, NG * S5_BATCH, D_S5), lambda b, t: (t, b, 0)),
        out_shape=jax.ShapeDtypeStruct((L, B, D_S5), F32),
        scratch_shapes=[
            pltpu.VMEM((NG, S5_BATCH, n_state), F32),
            pltpu.VMEM((NG, T * S5_BATCH, n_state), F32),
            pltpu.VMEM((NG, T * S5_BATCH, n_state), BF16),
        ],
        compiler_params=pltpu.CompilerParams(
            dimension_semantics=("parallel", "arbitrary"), vmem_limit_bytes=VMEM_LIMIT),
        name="s5",
    )(u_tm, a_re, a_im, b4, c2, d_skip, w_glu, b_glu)


def _rwkv_kernel(ah_ref, rh_ref, bh_ref, kh_ref, v_ref, we_ref, bonus_ref, gate_ref, gnw_ref,
                 gnb_ref, o_ref, s_scr, *, C, Bt):
    @pl.when(pl.program_id(1) == 0)
    def _():
        s_scr[...] = jnp.zeros_like(s_scr)

    row = lax.broadcasted_iota(jnp.int32, (C, LANES), 0)
    lane = lax.broadcasted_iota(jnp.int32, (C, LANES), 1)
    col = lane & (RWKV_HEAD - 1)
    head0 = lane < RWKV_HEAD
    strict = col < row
    incl = col <= row
    eye = (col == row).astype(F32)
    blk = row ^ col
    ones_blk = _pair_ones()
    same_head = ones_blk > 0
    sl = [slice(p * LANES, (p + 1) * LANES) for p in range(RW_PAIRS)]
    bis = list(range(Bt))
    ch = [(bi, p) for bi in bis for p in range(RW_PAIRS)]

    def bd(y):
        zero = jnp.zeros_like(y)
        return jnp.concatenate([jnp.where(head0, y, zero), jnp.where(head0, zero, y)], axis=0)

    def bmm(x, y):
        return _dot(x, bd(y))

    nt = (((1,), (1,)), ((), ()))
    tn = (((0,), (0,)), ((), ()))

    def each(fn, *lists):
        return [fn(*args) for args in zip(*lists)]

    ahp = [ah_ref[bi][:, sl[p]] for bi, p in ch]
    rhp = [rh_ref[bi][:, sl[p]] for bi, p in ch]
    bhp = [bh_ref[bi][:, sl[p]] for bi, p in ch]
    khp = [kh_ref[bi][:, sl[p]] for bi, p in ch]
    vp = [v_ref[bi][:, sl[p]] for bi, p in ch]
    wep = [we_ref[bi, 0][:, sl[p]] for bi, p in ch]
    S = [s_scr[bi, p] for bi, p in ch]

    zero = jnp.zeros((C, LANES), BF16)
    lhs = each(lambda x, y: jnp.concatenate([x, y], axis=0), ahp, rhp)
    lhs_m = each(lambda x, y: jnp.concatenate(
        [jnp.where(head0, x, zero), jnp.where(head0, zero, x),
         jnp.where(head0, y, zero), jnp.where(head0, zero, y)], axis=0), ahp, rhp)
    rhs = each(lambda x, y: jnp.concatenate([x, y], axis=0), bhp, khp)
    gram = each(lambda x, y: lax.dot_general(x, y, nt, preferred_element_type=F32), lhs_m, rhs)
    swap = lambda t: pltpu.roll(t, RWKV_HEAD, 1)
    a_ab = [jnp.where(strict, jnp.where(head0, t[:C], swap(t[C:2 * C])), 0.0) for t in gram]
    a_ak = [jnp.where(strict, jnp.where(head0, swap(t[:C]), t[C:2 * C]), 0.0) for t in gram]
    r_b = [jnp.where(incl, jnp.where(head0, t[2 * C:3 * C], swap(t[3 * C:])), 0.0) for t in gram]
    r_k = [jnp.where(incl, jnp.where(head0, swap(t[2 * C:3 * C]), t[3 * C:]), 0.0) for t in gram]
    ss = each(lambda x, st: lax.dot_general(x, st.astype(BF16), nt, preferred_element_type=F32),
              lhs, S)
    akv = each(lambda aa, rk_, vv: bmm(jnp.concatenate([aa, rk_], axis=0), vv), a_ak, r_k, vp)
    def neumann3(q):
        q2 = each(bmm, q, q)
        iq = [eye + t for t in q]
        both = each(lambda x, y: bmm(jnp.concatenate([x, y], axis=0), x), q2, iq)
        return each(lambda x, y: x + y[C:], iq, both), [t[:C] for t in both]

    t8, p4 = neumann3([jnp.where(blk < 8, t, 0.0) for t in a_ab])
    t8 = each(lambda x, y: x + bmm(x, y), t8, p4)
    n1 = each(bmm, t8, [jnp.where(blk < 8, 0.0, t) for t in a_ab])
    y = each(lambda t, s_, kv: bmm(t, s_[:C] + kv[:C]), t8, ss, akv)
    n123, n4 = neumann3(n1)
    z = each(lambda x, w: x + bmm(w, x), y, n4)
    u = each(bmm, n123, z)
    o = each(lambda s_, t, rb, uu: s_[C:] + t[C:] + bmm(rb, uu), ss, akv, r_b, u)
    upd = each(lambda uu, vv, bb, we: lax.dot_general(
        jnp.concatenate([uu.astype(BF16), vv], axis=0), (bb * we).astype(BF16), tn,
        preferred_element_type=F32), u, vp, rhs, wep)
    for (bi, p), st, we, up in zip(ch, S, wep, upd):
        s_scr[bi, p] = st * we + jnp.where(same_head, up, 0.0)
    o = jnp.concatenate(
        [jnp.concatenate(o[bi * RW_PAIRS:(bi + 1) * RW_PAIRS], axis=-1) for bi in bis], axis=0)
    dlt = o - _headsum(o, ones_blk) * (1.0 / RWKV_HEAD)
    var = _headsum(dlt * dlt, ones_blk) * (1.0 / RWKV_HEAD)
    on = dlt * lax.rsqrt(var + GN_EPS) * gnw_ref[...] + gnb_ref[...]
    rows = Bt * C
    out = (on + bonus_ref[...].reshape(rows, D_RWKV)) * gate_ref[...].reshape(rows, D_RWKV)
    o_ref[...] = out.reshape(Bt, C, D_RWKV)


def _rwkv(ah, rh, bh, kh, v, we, bonus, gate, gn_w, gn_b, *, Bt=RW_BT):
    B, L, _ = ah.shape
    C = RW_CHUNK
    Bt = min(Bt, B)
    seq = lambda: pl.BlockSpec((Bt, C, D_RWKV), lambda b, c: (b, c, 0))
    const = lambda *shape: pl.BlockSpec(shape, lambda b, c: (0,) * len(shape))
    return pl.pallas_call(
        functools.partial(_rwkv_kernel, C=C, Bt=Bt),
        grid=(B // Bt, L // C),
        in_specs=[
            seq(), seq(), seq(), seq(), seq(),
            pl.BlockSpec((Bt, 1, 1, D_RWKV), lambda b, c: (b, c, 0, 0)),
            seq(), seq(),
            const(1, D_RWKV),
            const(1, D_RWKV),
        ],
        out_specs=seq(),
        out_shape=jax.ShapeDtypeStruct((B, L, D_RWKV), F32),
        scratch_shapes=[pltpu.VMEM((Bt, RW_PAIRS, LANES, LANES), F32)],
        compiler_params=pltpu.CompilerParams(
            dimension_semantics=("parallel", "arbitrary"), vmem_limit_bytes=VMEM_LIMIT),
        name="rwkv7",
    )(ah, rh, bh, kh, v, we, bonus, gate, gn_w, gn_b)


def _ffn_kernel(ys_ref, yr_ref, x_ref, wo_ref, gpm_ref, gpre_ref, wup_ref, wdn_ref, gpost_ref,
                o_ref):
    n = ys_ref.shape[0] // 2
    halves = [slice(0, n), slice(n, 2 * n)]
    mix = [_dot(ys_ref[s, :], wo_ref[:D_S5, :]) + _dot(yr_ref[0, s, :], wo_ref[D_S5:, :])
           for s in halves]
    h = [x_ref[0, s, :] + _rms(m, gpm_ref[...]) for s, m in zip(halves, mix)]
    up = [_dot(_rms(t, gpre_ref[...]), wup_ref[...]) for t in h]
    ff = [_dot(jnp.square(jnp.maximum(t, 0.0)), wdn_ref[...]) for t in up]
    for s, t, f in zip(halves, h, ff):
        o_ref[0, s, :] = t + _rms(f, gpost_ref[...])


def _ffn(ys_tm, yr, x, w_out, g_post_mix, g_pre_mlp, w_up, w_down, g_post_mlp):
    B, L, _ = x.shape
    tl = min(ROW_TILE, L)
    const = lambda *shape: pl.BlockSpec(shape, lambda b, i: (0,) * len(shape),
                                        pipeline_mode=pl.Buffered(1))
    return pl.pallas_call(
        _ffn_kernel,
        grid=(B, L // tl),
        in_specs=[
            pl.BlockSpec((tl, D_S5), lambda b, i: (i, b)),
            pl.BlockSpec((1, tl, D_RWKV), lambda b, i: (b, i, 0)),
            pl.BlockSpec((1, tl, D_MODEL), lambda b, i: (b, i, 0)),
            const(D_MODEL, D_MODEL),
            const(1, D_MODEL),
            const(1, D_MODEL),
            const(D_MODEL, D_FF),
            const(D_FF, D_MODEL),
            const(1, D_MODEL),
        ],
        out_specs=pl.BlockSpec((1, tl, D_MODEL), lambda b, i: (b, i, 0)),
        out_shape=jax.ShapeDtypeStruct((B, L, D_MODEL), F32),
        compiler_params=pltpu.CompilerParams(
            dimension_semantics=("parallel", "parallel"), vmem_limit_bytes=VMEM_LIMIT),
        name="ffn",
    )(ys_tm.reshape(L, B * D_S5), yr, x, w_out, g_post_mix, g_pre_mlp, w_up, w_down, g_post_mlp)


def _layer(x, g_pre_mix, w_in, s5_lam_re, s5_lam_im, s5_log_dt, s5_b_re, s5_b_im, s5_c_re, s5_c_im,
           s5_d, s5_w_glu, s5_b_glu, rw_mu, rw_w0, rw_w_up, rw_a0, rw_a_up, rw_g_up, rw_k_k, rw_k_a,
           rw_r_k, rw_gn_w, rw_gn_b, w_out, g_post_mix, g_pre_mlp, w_ff_up, w_ff_down, g_post_mlp):
    row = lambda t: t.reshape(1, -1)
    zeros = jnp.zeros((W_RANK, D_RWKV), F32)
    wwa = jnp.concatenate([jnp.concatenate([rw_w_up, zeros], axis=1),
                           jnp.concatenate([zeros, rw_a_up], axis=1)], axis=0).astype(BF16)
    u_tm, *rw_ops = _inproj(x, row(g_pre_mix), w_in.astype(BF16), row(rw_mu), row(rw_w0), wwa,
                            row(rw_a0), rw_g_up.astype(BF16), row(rw_k_k), row(rw_k_a), row(rw_r_k))
    a_re, a_im, b4, c2 = _s5_params(s5_lam_re, s5_lam_im, s5_log_dt, s5_b_re, s5_b_im,
                                    s5_c_re, s5_c_im)
    ys_tm = _s5(u_tm, a_re, a_im, b4, c2, row(s5_d), s5_w_glu.astype(BF16), row(s5_b_glu))
    yr = _rwkv(*rw_ops, row(rw_gn_w), row(rw_gn_b))
    return _ffn(ys_tm, yr, x, w_out.astype(BF16), row(g_post_mix), row(g_pre_mlp),
                w_ff_up.astype(BF16), w_ff_down.astype(BF16), row(g_post_mlp))


def kernel(x, g_pre_mix, w_in, s5_lam_re, s5_lam_im, s5_log_dt, s5_b_re, s5_b_im, s5_c_re, s5_c_im,
           s5_d, s5_w_glu, s5_b_glu, rw_mu, rw_w0, rw_w_up, rw_a0, rw_a_up, rw_g_up, rw_k_k, rw_k_a,
           rw_r_k, rw_gn_w, rw_gn_b, w_out, g_post_mix, g_pre_mlp, w_ff_up, w_ff_down, g_post_mlp):
    params = (g_pre_mix, w_in, s5_lam_re, s5_lam_im, s5_log_dt, s5_b_re, s5_b_im, s5_c_re, s5_c_im,
              s5_d, s5_w_glu, s5_b_glu, rw_mu, rw_w0, rw_w_up, rw_a0, rw_a_up, rw_g_up, rw_k_k,
              rw_k_a, rw_r_k, rw_gn_w, rw_gn_b, w_out, g_post_mix, g_pre_mlp, w_ff_up, w_ff_down,
              g_post_mlp)
    h = x
    for layer in range(g_pre_mix.shape[0]):
        h = _layer(h, *[p[layer] for p in params])
    return h
```

```python
import functools
import math

import jax
import jax.numpy as jnp
from jax import lax
from jax.experimental import pallas as pl
from jax.experimental.pallas import tpu as pltpu

F32 = jnp.float32
BF16 = jnp.bfloat16

D_MODEL = 1024
D_S5 = 512
D_RWKV = 512
S5_GROUP = 16
S5_GROUPS = 32
S5_STATE = 64
RWKV_HEAD = 64
RWKV_HEADS = 8
W_RANK = 64
A_RANK = 64
G_RANK = 128
N_RW_COLS = 3 * D_RWKV + W_RANK + A_RANK + G_RANK
N_PROJ = D_S5 + N_RW_COLS
D_FF = 4 * D_MODEL
RMS_EPS = 1e-6
GN_EPS = 64e-5

SUBLANES = 8
LANES = 128

ROW_TILE = 512
IN_BATCH = 16
S5_BATCH = SUBLANES
S5_T = 64
S5_GROUPS_PER_STEP = 2
S5_BLOCK_GROUPS = LANES // S5_GROUP
S5_BLOCKS = S5_GROUPS // S5_BLOCK_GROUPS
S5_BLOCK_STATE = S5_BLOCK_GROUPS * S5_STATE
S5_BLOCK_COLS = 2 * S5_BLOCK_STATE
RW_CHUNK = 64
RW_PAIRS = RWKV_HEADS // 2
RW_BT = 16
VMEM_LIMIT = 56 * 1024 * 1024


def _rms(x, g):
    return x * lax.rsqrt(jnp.mean(x * x, axis=-1, keepdims=True) + RMS_EPS) * g


def _dot(a, b):
    return jnp.dot(a.astype(BF16), b.astype(BF16), preferred_element_type=F32)


def _pair_ones():
    r = lax.broadcasted_iota(jnp.int32, (LANES, LANES), 0)
    c = lax.broadcasted_iota(jnp.int32, (LANES, LANES), 1)
    return ((r < RWKV_HEAD) == (c < RWKV_HEAD)).astype(BF16)


def _headsum(x, ones_blk):
    n = x.shape[0]
    tall = jnp.concatenate([x[:, p * LANES:(p + 1) * LANES] for p in range(RW_PAIRS)], axis=0)
    t = jnp.dot(tall.astype(BF16), ones_blk, preferred_element_type=F32)
    return jnp.concatenate([t[p * n:(p + 1) * n] for p in range(RW_PAIRS)], axis=-1)


def _inproj_kernel(x_ref, g_ref, w_ref, mu_ref, w0_ref, wwa_ref, a0_ref, gup_ref, kk_ref, ka_ref,
                   rk_ref, u_ref, ah_ref, rh_ref, bh_ref, kh_ref, v_ref, we_ref, bonus_ref, gate_ref,
                   prev_scr, *, nb, T):
    @pl.when(pl.program_id(1) == 0)
    def _():
        prev_scr[...] = jnp.zeros_like(prev_scr)

    rows = nb * T
    xn = _rms(x_ref[...].reshape(rows, D_MODEL), g_ref[...]).astype(BF16)
    first = lax.broadcasted_iota(jnp.int32, (SUBLANES, LANES), 0) == 0

    def project(c0, c1):
        q = jnp.dot(xn, w_ref[:, D_S5 + c0:D_S5 + c1], preferred_element_type=F32)
        rolled = pltpu.roll(q, 1, 0)
        edge = jnp.concatenate([first] * ((c1 - c0) // LANES), axis=1)
        pieces = []
        for b in range(nb):
            pieces.append(jnp.where(edge, prev_scr[b, :, c0:c1], rolled[b * T:b * T + SUBLANES]))
            pieces.append(rolled[b * T + SUBLANES:(b + 1) * T])
            prev_scr[b, :, c0:c1] = q[(b + 1) * T - 1:(b + 1) * T, :]
        prev = jnp.concatenate(pieces, axis=0)
        return q + mu_ref[:, c0:c1] * (prev - q)

    ones_blk = _pair_ones()
    xw_xa_xg = project(3 * D_RWKV, N_RW_COLS)
    xwa = xw_xa_xg[:, :LANES]
    k = project(D_RWKV, 2 * D_RWKV)
    head0r = lax.broadcasted_iota(jnp.int32, (rows, LANES), 1) < RWKV_HEAD
    wa = _dot(jnp.where(head0r, jnp.tanh(xwa), xwa), wwa_ref[...])
    gate = _dot(jax.nn.sigmoid(xw_xa_xg[:, LANES:]), gup_ref[...])
    r = project(0, D_RWKV)
    kk = k * kk_ref[...]
    kk = kk / jnp.maximum(jnp.sqrt(_headsum(kk * kk, ones_blk)), 1e-12)
    v = project(2 * D_RWKV, 3 * D_RWKV)
    lw = -math.exp(-0.5) * jax.nn.sigmoid(w0_ref[...] + wa[:, :D_RWKV])
    a = jax.nn.sigmoid(a0_ref[...] + wa[:, D_RWKV:])
    k2 = k * (1.0 + (a - 1.0) * ka_ref[...])
    tril = (lax.broadcasted_iota(jnp.int32, (T, T), 0)
            >= lax.broadcasted_iota(jnp.int32, (T, T), 1)).astype(BF16)
    lw_hi = lw.astype(BF16)
    lw_lo = (lw - lw_hi.astype(F32)).astype(BF16)
    cum = jnp.concatenate(
        [jnp.dot(tril, lw_hi[b * T:(b + 1) * T], preferred_element_type=F32)
         + jnp.dot(tril, lw_lo[b * T:(b + 1) * T], preferred_element_type=F32) for b in range(nb)],
        axis=0)
    u = jnp.dot(xn, w_ref[:, :D_S5], preferred_element_type=F32)
    for b in range(nb):
        u_ref[:, b, :] = u[b * T:(b + 1) * T, :]
    bonus = _headsum(r * k2 * rk_ref[...], ones_blk) * v
    w_in = jnp.exp(cum)
    w_inv = jnp.exp(-cum)
    shape = (nb, T, D_RWKV)
    ah_ref[...] = (-kk * jnp.exp(cum - lw)).astype(BF16).reshape(shape)
    rh_ref[...] = (r * w_in).astype(BF16).reshape(shape)
    bh_ref[...] = (kk * a * w_inv).astype(BF16).reshape(shape)
    kh_ref[...] = (k2 * w_inv).astype(BF16).reshape(shape)
    v_ref[...] = v.astype(BF16).reshape(shape)
    bonus_ref[...] = bonus.reshape(shape)
    gate_ref[...] = gate.reshape(shape)
    for b in range(nb):
        we_ref[b, 0] = w_in[(b + 1) * T - 1:(b + 1) * T, :]


def _inproj(x, g, w_in, mu, w0, wwa, a0, g_up, k_k, k_a, r_k):
    B, L, _ = x.shape
    nb = min(IN_BATCH, B)
    T = RW_CHUNK
    const = lambda *shape: pl.BlockSpec(shape, lambda b, i: (0,) * len(shape))
    seq = lambda: pl.BlockSpec((nb, T, D_RWKV), lambda b, i: (b, i, 0))
    bf = jax.ShapeDtypeStruct((B, L, D_RWKV), BF16)
    f32 = jax.ShapeDtypeStruct((B, L, D_RWKV), F32)
    return pl.pallas_call(
        functools.partial(_inproj_kernel, nb=nb, T=T),
        grid=(B // nb, L // T),
        in_specs=[
            pl.BlockSpec((nb, T, D_MODEL), lambda b, i: (b, i, 0)),
            const(1, D_MODEL),
            const(D_MODEL, N_PROJ),
            const(1, N_RW_COLS),
            const(1, D_RWKV),
            const(LANES, 2 * D_RWKV),
            const(1, D_RWKV),
            const(G_RANK, D_RWKV),
            const(1, D_RWKV),
            const(1, D_RWKV),
            const(1, D_RWKV),
        ],
        out_specs=[
            pl.BlockSpec((T, nb, D_S5), lambda b, i: (i, b, 0)),
            seq(), seq(), seq(), seq(), seq(),
            pl.BlockSpec((nb, 1, 1, D_RWKV), lambda b, i: (b, i, 0, 0)),
            seq(), seq(),
        ],
        out_shape=[
            jax.ShapeDtypeStruct((L, B, D_S5), F32),
            bf, bf, bf, bf, bf,
            jax.ShapeDtypeStruct((B, L // T, 1, D_RWKV), F32),
            f32, f32,
        ],
        scratch_shapes=[pltpu.VMEM((nb, 1, N_RW_COLS), F32)],
        compiler_params=pltpu.CompilerParams(
            dimension_semantics=("parallel", "arbitrary"), vmem_limit_bytes=VMEM_LIMIT),
        name="inproj",
    )(x, g, w_in, mu, w0, wwa, a0, g_up, k_k, k_a, r_k)


def _s5_kernel(u_ref, are_ref, aim_ref, b4_ref, c2_ref, d_ref, wglu_ref, bglu_ref,
               y_ref, s_scr, bu_scr, sb_scr, *, T, NG):
    rows = T * S5_BATCH
    pair_cols = 2 * S5_BLOCK_COLS

    @pl.when(pl.program_id(1) == 0)
    def _():
        s_scr[...] = jnp.zeros_like(s_scr)

    us = [u_ref[:, g * S5_BATCH:(g + 1) * S5_BATCH, :].reshape(rows, D_S5) for g in range(NG)]
    ubs = [u.astype(BF16) for u in us]

    def bu(g, h):
        for q in (2 * h, 2 * h + 1):
            bu_scr[g, :, q * S5_BLOCK_COLS:(q + 1) * S5_BLOCK_COLS] = jnp.dot(
                ubs[g][:, q * LANES:(q + 1) * LANES], b4_ref[q], preferred_element_type=F32)

    def scan(g, h):
        qs = (2 * h, 2 * h + 1)
        a_bc = {}
        st = {}
        for q in qs:
            a0 = q * S5_BLOCK_STATE
            a_bc[q] = (
                jnp.broadcast_to(are_ref[:, a0:a0 + S5_BLOCK_STATE], (S5_BATCH, S5_BLOCK_STATE)),
                jnp.broadcast_to(aim_ref[:, a0:a0 + S5_BLOCK_STATE], (S5_BATCH, S5_BLOCK_STATE)))
            re0 = q * S5_BLOCK_COLS
            st[q] = (s_scr[g, :, re0:re0 + S5_BLOCK_STATE],
                     s_scr[g, :, re0 + S5_BLOCK_STATE:re0 + S5_BLOCK_COLS])
        held = {}
        for t in range(T):
            r0 = t * S5_BATCH
            for q in qs:
                re0 = q * S5_BLOCK_COLS
                im0 = re0 + S5_BLOCK_STATE
                are, aim = a_bc[q]
                sre, sim = st[q]
                nre = are * sre - aim * sim + bu_scr[g, r0:r0 + S5_BATCH, re0:re0 + S5_BLOCK_STATE]
                nim = are * sim + aim * sre + bu_scr[g, r0:r0 + S5_BATCH, im0:im0 + S5_BLOCK_STATE]
                st[q] = (nre, nim)
                if t % 2 == 0:
                    held[q] = (nre, nim)
                else:
                    p0 = r0 - S5_BATCH
                    sb_scr[g, p0:p0 + 2 * S5_BATCH, re0:re0 + S5_BLOCK_STATE] = jnp.concatenate(
                        [held[q][0], nre], axis=0).astype(BF16)
                    sb_scr[g, p0:p0 + 2 * S5_BATCH, im0:im0 + S5_BLOCK_STATE] = jnp.concatenate(
                        [held[q][1], nim], axis=0).astype(BF16)
        for q in qs:
            re0 = q * S5_BLOCK_COLS
            s_scr[g, :, re0:re0 + S5_BLOCK_STATE] = st[q][0]
            s_scr[g, :, re0 + S5_BLOCK_STATE:re0 + S5_BLOCK_COLS] = st[q][1]

    def cs(g, h):
        return jnp.dot(sb_scr[g, :, h * pair_cols:(h + 1) * pair_cols], c2_ref[h],
                       preferred_element_type=F32)

    def gelu_in(g, ys):
        return jax.nn.gelu(jnp.concatenate(ys, axis=-1) + d_ref[...] * us[g])

    def glu_out(g, z):
        gate = jax.nn.sigmoid(_dot(z, wglu_ref[...]) + bglu_ref[...])
        y_ref[:, g * S5_BATCH:(g + 1) * S5_BATCH, :] = (z * gate).reshape(T, S5_BATCH, D_S5)

    n_pairs = S5_BLOCKS // 2
    for h in range(n_pairs):
        bu(0, h)
    ys = {g: [] for g in range(NG)}
    for g in range(NG):
        for h in range(n_pairs):
            scan(g, h)
            if g + 1 < NG:
                bu(g + 1, h)
            ys[g].append(cs(g, h))
            if g > 0 and h == 0:
                glu_out(g - 1, gelu_in(g - 1, ys[g - 1]))
    glu_out(NG - 1, gelu_in(NG - 1, ys[NG - 1]))


def _s5_params(lam_re, lam_im, log_dt, b_re, b_im, c_re, c_im):
    dt = jnp.exp(log_dt)[:, None]
    mag = jnp.exp(lam_re * dt)
    ab_re = mag * jnp.cos(lam_im * dt)
    ab_im = mag * jnp.sin(lam_im * dt)
    den = lam_re * lam_re + lam_im * lam_im
    nr = ab_re - 1.0
    f_re = (nr * lam_re + ab_im * lam_im) / den
    f_im = (ab_im * lam_re - nr * lam_im) / den
    bb_re = f_re[..., None] * b_re - f_im[..., None] * b_im
    bb_im = f_re[..., None] * b_im + f_im[..., None] * b_re
    eye = jnp.eye(S5_BLOCK_GROUPS, dtype=F32)

    def blk(t, q):
        return t[q * S5_BLOCK_GROUPS:(q + 1) * S5_BLOCK_GROUPS]

    def b_block(bb, q):
        return jnp.einsum("gpc,gk->gckp", blk(bb, q), eye).reshape(LANES, S5_BLOCK_STATE)

    def c_block(cc, q):
        return jnp.einsum("gcp,gk->gpkc", blk(cc, q), eye).reshape(S5_BLOCK_STATE, LANES)

    b4 = jnp.stack([jnp.concatenate([b_block(bb_re, q), b_block(bb_im, q)], axis=1)
                    for q in range(S5_BLOCKS)]).astype(BF16)
    c4 = [jnp.concatenate([c_block(c_re, q), -c_block(c_im, q)], axis=0)
          for q in range(S5_BLOCKS)]
    zc = jnp.zeros_like(c4[0])
    c2 = jnp.stack([jnp.concatenate([jnp.concatenate([c4[2 * h], zc], axis=1),
                                     jnp.concatenate([zc, c4[2 * h + 1]], axis=1)], axis=0)
                    for h in range(S5_BLOCKS // 2)]).astype(BF16)
    return ab_re.reshape(1, -1), ab_im.reshape(1, -1), b4, c2


def _s5(u_tm, a_re, a_im, b4, c2, d_skip, w_glu, b_glu):
    L, B, _ = u_tm.shape
    T = min(S5_T, L)
    NG = min(S5_GROUPS_PER_STEP, B // S5_BATCH)
    n_state = S5_BLOCKS * S5_BLOCK_COLS
    const = lambda *shape: pl.BlockSpec(shape, lambda b, t: (0,) * len(shape))
    return pl.pallas_call(
        functools.partial(_s5_kernel, T=T, NG=NG),
        grid=(B // (NG * S5_BATCH), L // T),
        in_specs=[
            pl.BlockSpec((T, NG * S5_BATCH, D_S5), lambda b, t: (t, b, 0)),
            const(1, S5_GROUPS * S5_STATE),
            const(1, S5_GROUPS * S5_STATE),
            const(S5_BLOCKS, LANES, S5_BLOCK_COLS),
            const(S5_BLOCKS // 2, 2 * S5_BLOCK_COLS, 2 * LANES),
            const(1, D_S5),
            const(D_S5, D_S5),
            const(1, D_S5),
        ],
        out_specs=pl.BlockSpec((T, NG * S5_BATCH, D_S5), lambda b, t: (t, b, 0)),
        out_shape=jax.ShapeDtypeStruct((L, B, D_S5), F32),
        scratch_shapes=[
            pltpu.VMEM((NG, S5_BATCH, n_state), F32),
            pltpu.VMEM((NG, T * S5_BATCH, n_state), F32),
            pltpu.VMEM((NG, T * S5_BATCH, n_state), BF16),
        ],
        compiler_params=pltpu.CompilerParams(
            dimension_semantics=("parallel", "arbitrary"), vmem_limit_bytes=VMEM_LIMIT),
        name="s5",
    )(u_tm, a_re, a_im, b4, c2, d_skip, w_glu, b_glu)


def _rwkv_kernel(ah_ref, rh_ref, bh_ref, kh_ref, v_ref, we_ref, bonus_ref, gate_ref, gnw_ref,
                 gnb_ref, o_ref, s_scr, *, C, Bt):
    @pl.when(pl.program_id(1) == 0)
    def _():
        s_scr[...] = jnp.zeros_like(s_scr)

    row = lax.broadcasted_iota(jnp.int32, (C, LANES), 0)
    lane = lax.broadcasted_iota(jnp.int32, (C, LANES), 1)
    col = lane & (RWKV_HEAD - 1)
    head0 = lane < RWKV_HEAD
    strict = col < row
    incl = col <= row
    eye = (col == row).astype(F32)
    blk = row ^ col
    ones_blk = _pair_ones()
    same_head = ones_blk > 0
    sl = [slice(p * LANES, (p + 1) * LANES) for p in range(RW_PAIRS)]
    bis = list(range(Bt))
    ch = [(bi, p) for bi in bis for p in range(RW_PAIRS)]

    def bd(y):
        zero = jnp.zeros_like(y)
        return jnp.concatenate([jnp.where(head0, y, zero), jnp.where(head0, zero, y)], axis=0)

    def bmm(x, y):
        return _dot(x, bd(y))

    nt = (((1,), (1,)), ((), ()))
    tn = (((0,), (0,)), ((), ()))

    def each(fn, *lists):
        return [fn(*args) for args in zip(*lists)]

    ahp = [ah_ref[bi][:, sl[p]] for bi, p in ch]
    rhp = [rh_ref[bi][:, sl[p]] for bi, p in ch]
    bhp = [bh_ref[bi][:, sl[p]] for bi, p in ch]
    khp = [kh_ref[bi][:, sl[p]] for bi, p in ch]
    vp = [v_ref[bi][:, sl[p]] for bi, p in ch]
    wep = [we_ref[bi, 0][:, sl[p]] for bi, p in ch]
    S = [s_scr[bi, p] for bi, p in ch]

    zero = jnp.zeros((C, LANES), BF16)
    lhs = each(lambda x, y: jnp.concatenate([x, y], axis=0), ahp, rhp)
    lhs_m = each(lambda x, y: jnp.concatenate(
        [jnp.where(head0, x, zero), jnp.where(head0, zero, x),
         jnp.where(head0, y, zero), jnp.where(head0, zero, y)], axis=0), ahp, rhp)
    rhs = each(lambda x, y: jnp.concatenate([x, y], axis=0), bhp, khp)
    gram = each(lambda x, y: lax.dot_general(x, y, nt, preferred_element_type=F32), lhs_m, rhs)
    swap = lambda t: pltpu.roll(t, RWKV_HEAD, 1)
    a_ab = [jnp.where(strict, jnp.where(head0, t[:C], swap(t[C:2 * C])), 0.0) for t in gram]
    a_ak = [jnp.where(strict, jnp.where(head0, swap(t[:C]), t[C:2 * C]), 0.0) for t in gram]
    r_b = [jnp.where(incl, jnp.where(head0, t[2 * C:3 * C], swap(t[3 * C:])), 0.0) for t in gram]
    r_k = [jnp.where(incl, jnp.where(head0, swap(t[2 * C:3 * C]), t[3 * C:]), 0.0) for t in gram]
    ss = each(lambda x, st: lax.dot_general(x, st.astype(BF16), nt, preferred_element_type=F32),
              lhs, S)
    akv = each(lambda aa, rk_, vv: bmm(jnp.concatenate([aa, rk_], axis=0), vv), a_ak, r_k, vp)
    def neumann3(q):
        q2 = each(bmm, q, q)
        iq = [eye + t for t in q]
        both = each(lambda x, y: bmm(jnp.concatenate([x, y], axis=0), x), q2, iq)
        return each(lambda x, y: x + y[C:], iq, both), [t[:C] for t in both]

    t8, p4 = neumann3([jnp.where(blk < 8, t, 0.0) for t in a_ab])
    t8 = each(lambda x, y: x + bmm(x, y), t8, p4)
    n1 = each(bmm, t8, [jnp.where(blk < 8, 0.0, t) for t in a_ab])
    y = each(lambda t, s_, kv: bmm(t, s_[:C] + kv[:C]), t8, ss, akv)
    n123, n4 = neumann3(n1)
    z = each(lambda x, w: x + bmm(w, x), y, n4)
    u = each(bmm, n123, z)
    o = each(lambda s_, t, rb, uu: s_[C:] + t[C:] + bmm(rb, uu), ss, akv, r_b, u)
    upd = each(lambda uu, vv, bb, we: lax.dot_general(
        jnp.concatenate([uu.astype(BF16), vv], axis=0), (bb * we).astype(BF16), tn,
        preferred_element_type=F32), u, vp, rhs, wep)
    for (bi, p), st, we, up in zip(ch, S, wep, upd):
        s_scr[bi, p] = st * we + jnp.where(same_head, up, 0.0)
    o = jnp.concatenate(
        [jnp.concatenate(o[bi * RW_PAIRS:(bi + 1) * RW_PAIRS], axis=-1) for bi in bis], axis=0)
    dlt = o - _headsum(o, ones_blk) * (1.0 / RWKV_HEAD)
    var = _headsum(dlt * dlt, ones_blk) * (1.0 / RWKV_HEAD)
    on = dlt * lax.rsqrt(var + GN_EPS) * gnw_ref[...] + gnb_ref[...]
    rows = Bt * C
    out = (on + bonus_ref[...].reshape(rows, D_RWKV)) * gate_ref[...].reshape(rows, D_RWKV)
    o_ref[...] = out.reshape(Bt, C, D_RWKV)


def _rwkv(ah, rh, bh, kh, v, we, bonus, gate, gn_w, gn_b, *, Bt=RW_BT):
    B, L, _ = ah.shape
    C = RW_CHUNK
    Bt = min(Bt, B)
    seq = lambda: pl.BlockSpec((Bt, C, D_RWKV), lambda b, c: (b, c, 0))
    const = lambda *shape: pl.BlockSpec(shape, lambda b, c: (0,) * len(shape))
    return pl.pallas_call(
        functools.partial(_rwkv_kernel, C=C, Bt=Bt),
        grid=(B // Bt, L // C),
        in_specs=[
            seq(), seq(), seq(), seq(), seq(),
            pl.BlockSpec((Bt, 1, 1, D_RWKV), lambda b, c: (b, c, 0, 0)),
            seq(), seq(),
            const(1, D_RWKV),
            const(1, D_RWKV),
        ],
        out_specs=seq(),
        out_shape=jax.ShapeDtypeStruct((B, L, D_RWKV), F32),
        scratch_shapes=[pltpu.VMEM((Bt, RW_PAIRS, LANES, LANES), F32)],
        compiler_params=pltpu.CompilerParams(
            dimension_semantics=("parallel", "arbitrary"), vmem_limit_bytes=VMEM_LIMIT),
        name="rwkv7",
    )(ah, rh, bh, kh, v, we, bonus, gate, gn_w, gn_b)


def _ffn_kernel(ys_ref, yr_ref, x_ref, wo_ref, gpm_ref, gpre_ref, wup_ref, wdn_ref, gpost_ref,
                o_ref):
    n = ys_ref.shape[0] // 2
    halves = [slice(0, n), slice(n, 2 * n)]
    mix = [_dot(ys_ref[s, :], wo_ref[:D_S5, :]) + _dot(yr_ref[0, s, :], wo_ref[D_S5:, :])
           for s in halves]
    h = [x_ref[0, s, :] + _rms(m, gpm_ref[...]) for s, m in zip(halves, mix)]
    up = [_dot(_rms(t, gpre_ref[...]), wup_ref[...]) for t in h]
    ff = [_dot(jnp.square(jnp.maximum(t, 0.0)), wdn_ref[...]) for t in up]
    for s, t, f in zip(halves, h, ff):
        o_ref[0, s, :] = t + _rms(f, gpost_ref[...])


def _ffn(ys_tm, yr, x, w_out, g_post_mix, g_pre_mlp, w_up, w_down, g_post_mlp):
    B, L, _ = x.shape
    tl = min(ROW_TILE, L)
    const = lambda *shape: pl.BlockSpec(shape, lambda b, i: (0,) * len(shape),
                                        pipeline_mode=pl.Buffered(1))
    return pl.pallas_call(
        _ffn_kernel,
        grid=(B, L // tl),
        in_specs=[
            pl.BlockSpec((tl, D_S5), lambda b, i: (i, b)),
            pl.BlockSpec((1, tl, D_RWKV), lambda b, i: (b, i, 0)),
            pl.BlockSpec((1, tl, D_MODEL), lambda b, i: (b, i, 0)),
            const(D_MODEL, D_MODEL),
            const(1, D_MODEL),
            const(1, D_MODEL),
            const(D_MODEL, D_FF),
            const(D_FF, D_MODEL),
            const(1, D_MODEL),
        ],
        out_specs=pl.BlockSpec((1, tl, D_MODEL), lambda b, i: (b, i, 0)),
        out_shape=jax.ShapeDtypeStruct((B, L, D_MODEL), F32),
        compiler_params=pltpu.CompilerParams(
            dimension_semantics=("parallel", "parallel"), vmem_limit_bytes=VMEM_LIMIT),
        name="ffn",
    )(ys_tm.reshape(L, B * D_S5), yr, x, w_out, g_post_mix, g_pre_mlp, w_up, w_down, g_post_mlp)


def _layer(x, g_pre_mix, w_in, s5_lam_re, s5_lam_im, s5_log_dt, s5_b_re, s5_b_im, s5_c_re, s5_c_im,
           s5_d, s5_w_glu, s5_b_glu, rw_mu, rw_w0, rw_w_up, rw_a0, rw_a_up, rw_g_up, rw_k_k, rw_k_a,
           rw_r_k, rw_gn_w, rw_gn_b, w_out, g_post_mix, g_pre_mlp, w_ff_up, w_ff_down, g_post_mlp):
    row = lambda t: t.reshape(1, -1)
    zeros = jnp.zeros((W_RANK, D_RWKV), F32)
    wwa = jnp.concatenate([jnp.concatenate([rw_w_up, zeros], axis=1),
                           jnp.concatenate([zeros, rw_a_up], axis=1)], axis=0).astype(BF16)
    u_tm, *rw_ops = _inproj(x, row(g_pre_mix), w_in.astype(BF16), row(rw_mu), row(rw_w0), wwa,
                            row(rw_a0), rw_g_up.astype(BF16), row(rw_k_k), row(rw_k_a), row(rw_r_k))
    a_re, a_im, b4, c2 = _s5_params(s5_lam_re, s5_lam_im, s5_log_dt, s5_b_re, s5_b_im,
                                    s5_c_re, s5_c_im)
    ys_tm = _s5(u_tm, a_re, a_im, b4, c2, row(s5_d), s5_w_glu.astype(BF16), row(s5_b_glu))
    yr = _rwkv(*rw_ops, row(rw_gn_w), row(rw_gn_b))
    return _ffn(ys_tm, yr, x, w_out.astype(BF16), row(g_post_mix), row(g_pre_mlp),
                w_ff_up.astype(BF16), w_ff_down.astype(BF16), row(g_post_mlp))


def kernel(x, g_pre_mix, w_in, s5_lam_re, s5_lam_im, s5_log_dt, s5_b_re, s5_b_im, s5_c_re, s5_c_im,
           s5_d, s5_w_glu, s5_b_glu, rw_mu, rw_w0, rw_w_up, rw_a0, rw_a_up, rw_g_up, rw_k_k, rw_k_a,
           rw_r_k, rw_gn_w, rw_gn_b, w_out, g_post_mix, g_pre_mlp, w_ff_up, w_ff_down, g_post_mlp):
    params = (g_pre_mix, w_in, s5_lam_re, s5_lam_im, s5_log_dt, s5_b_re, s5_b_im, s5_c_re, s5_c_im,
              s5_d, s5_w_glu, s5_b_glu, rw_mu, rw_w0, rw_w_up, rw_a0, rw_a_up, rw_g_up, rw_k_k,
              rw_k_a, rw_r_k, rw_gn_w, rw_gn_b, w_out, g_post_mix, g_pre_mlp, w_ff_up, w_ff_down,
              g_post_mlp)
    h = x
    for layer in range(g_pre_mix.shape[0]):
        h = _layer(h, *[p[layer] for p in params])
    return h
```

```python
import functools
import math

import jax
import jax.numpy as jnp
from jax import lax
from jax.experimental import pallas as pl
from jax.experimental.pallas import tpu as pltpu

F32 = jnp.float32
BF16 = jnp.bfloat16

D_MODEL = 1024
D_S5 = 512
D_RWKV = 512
S5_GROUP = 16
S5_GROUPS = 32
S5_STATE = 64
RWKV_HEAD = 64
RWKV_HEADS = 8
W_RANK = 64
A_RANK = 64
G_RANK = 128
N_RW_COLS = 3 * D_RWKV + W_RANK + A_RANK + G_RANK
N_PROJ = D_S5 + N_RW_COLS
D_FF = 4 * D_MODEL
RMS_EPS = 1e-6
GN_EPS = 64e-5

SUBLANES = 8
LANES = 128

ROW_TILE = 512
IN_BATCH = 16
S5_BATCH = SUBLANES
S5_T = 32
S5_GROUPS_PER_STEP = 4
S5_BLOCK_GROUPS = LANES // S5_GROUP
S5_BLOCKS = S5_GROUPS // S5_BLOCK_GROUPS
S5_BLOCK_STATE = S5_BLOCK_GROUPS * S5_STATE
S5_BLOCK_COLS = 2 * S5_BLOCK_STATE
RW_CHUNK = 64
RW_PAIRS = RWKV_HEADS // 2
RW_BT = 16
VMEM_LIMIT = 56 * 1024 * 1024


def _rms(x, g):
    return x * lax.rsqrt(jnp.mean(x * x, axis=-1, keepdims=True) + RMS_EPS) * g


def _dot(a, b):
    return jnp.dot(a.astype(BF16), b.astype(BF16), preferred_element_type=F32)


def _pair_ones():
    r = lax.broadcasted_iota(jnp.int32, (LANES, LANES), 0)
    c = lax.broadcasted_iota(jnp.int32, (LANES, LANES), 1)
    return ((r < RWKV_HEAD) == (c < RWKV_HEAD)).astype(BF16)


def _headsum(x, ones_blk):
    n = x.shape[0]
    tall = jnp.concatenate([x[:, p * LANES:(p + 1) * LANES] for p in range(RW_PAIRS)], axis=0)
    t = jnp.dot(tall.astype(BF16), ones_blk, preferred_element_type=F32)
    return jnp.concatenate([t[p * n:(p + 1) * n] for p in range(RW_PAIRS)], axis=-1)


def _inproj_kernel(x_ref, g_ref, w_ref, mu_ref, w0_ref, wwa_ref, a0_ref, gup_ref, kk_ref, ka_ref,
                   rk_ref, u_ref, ah_ref, rh_ref, bh_ref, kh_ref, v_ref, we_ref, bonus_ref, gate_ref,
                   prev_scr, *, nb, T):
    @pl.when(pl.program_id(1) == 0)
    def _():
        prev_scr[...] = jnp.zeros_like(prev_scr)

    rows = nb * T
    xn = _rms(x_ref[...].reshape(rows, D_MODEL), g_ref[...]).astype(BF16)
    first = lax.broadcasted_iota(jnp.int32, (SUBLANES, LANES), 0) == 0

    def project(c0, c1):
        q = jnp.dot(xn, w_ref[:, D_S5 + c0:D_S5 + c1], preferred_element_type=F32)
        rolled = pltpu.roll(q, 1, 0)
        edge = jnp.concatenate([first] * ((c1 - c0) // LANES), axis=1)
        pieces = []
        for b in range(nb):
            pieces.append(jnp.where(edge, prev_scr[b, :, c0:c1], rolled[b * T:b * T + SUBLANES]))
            pieces.append(rolled[b * T + SUBLANES:(b + 1) * T])
            prev_scr[b, :, c0:c1] = q[(b + 1) * T - 1:(b + 1) * T, :]
        prev = jnp.concatenate(pieces, axis=0)
        return q + mu_ref[:, c0:c1] * (prev - q)

    ones_blk = _pair_ones()
    xw_xa_xg = project(3 * D_RWKV, N_RW_COLS)
    xwa = xw_xa_xg[:, :LANES]
    k = project(D_RWKV, 2 * D_RWKV)
    head0r = lax.broadcasted_iota(jnp.int32, (rows, LANES), 1) < RWKV_HEAD
    wa = _dot(jnp.where(head0r, jnp.tanh(xwa), xwa), wwa_ref[...])
    gate = _dot(jax.nn.sigmoid(xw_xa_xg[:, LANES:]), gup_ref[...])
    r = project(0, D_RWKV)
    kk = k * kk_ref[...]
    kk = kk / jnp.maximum(jnp.sqrt(_headsum(kk * kk, ones_blk)), 1e-12)
    v = project(2 * D_RWKV, 3 * D_RWKV)
    lw = -math.exp(-0.5) * jax.nn.sigmoid(w0_ref[...] + wa[:, :D_RWKV])
    a = jax.nn.sigmoid(a0_ref[...] + wa[:, D_RWKV:])
    k2 = k * (1.0 + (a - 1.0) * ka_ref[...])
    tril = (lax.broadcasted_iota(jnp.int32, (T, T), 0)
            >= lax.broadcasted_iota(jnp.int32, (T, T), 1)).astype(BF16)
    lw_hi = lw.astype(BF16)
    lw_lo = (lw - lw_hi.astype(F32)).astype(BF16)
    cum = jnp.concatenate(
        [jnp.dot(tril, lw_hi[b * T:(b + 1) * T], preferred_element_type=F32)
         + jnp.dot(tril, lw_lo[b * T:(b + 1) * T], preferred_element_type=F32) for b in range(nb)],
        axis=0)
    u = jnp.dot(xn, w_ref[:, :D_S5], preferred_element_type=F32)
    for b in range(nb):
        u_ref[:, b, :] = u[b * T:(b + 1) * T, :]
    bonus = _headsum(r * k2 * rk_ref[...], ones_blk) * v
    w_in = jnp.exp(cum)
    w_inv = jnp.exp(-cum)
    shape = (nb, T, D_RWKV)
    ah_ref[...] = (-kk * jnp.exp(cum - lw)).astype(BF16).reshape(shape)
    rh_ref[...] = (r * w_in).astype(BF16).reshape(shape)
    bh_ref[...] = (kk * a * w_inv).astype(BF16).reshape(shape)
    kh_ref[...] = (k2 * w_inv).astype(BF16).reshape(shape)
    v_ref[...] = v.astype(BF16).reshape(shape)
    bonus_ref[...] = bonus.reshape(shape)
    gate_ref[...] = gate.reshape(shape)
    for b in range(nb):
        we_ref[b, 0] = w_in[(b + 1) * T - 1:(b + 1) * T, :]


def _inproj(x, g, w_in, mu, w0, wwa, a0, g_up, k_k, k_a, r_k):
    B, L, _ = x.shape
    nb = min(IN_BATCH, B)
    T = RW_CHUNK
    const = lambda *shape: pl.BlockSpec(shape, lambda b, i: (0,) * len(shape))
    seq = lambda: pl.BlockSpec((nb, T, D_RWKV), lambda b, i: (b, i, 0))
    bf = jax.ShapeDtypeStruct((B, L, D_RWKV), BF16)
    f32 = jax.ShapeDtypeStruct((B, L, D_RWKV), F32)
    return pl.pallas_call(
        functools.partial(_inproj_kernel, nb=nb, T=T),
        grid=(B // nb, L // T),
        in_specs=[
            pl.BlockSpec((nb, T, D_MODEL), lambda b, i: (b, i, 0)),
            const(1, D_MODEL),
            const(D_MODEL, N_PROJ),
            const(1, N_RW_COLS),
            const(1, D_RWKV),
            const(LANES, 2 * D_RWKV),
            const(1, D_RWKV),
            const(G_RANK, D_RWKV),
            const(1, D_RWKV),
            const(1, D_RWKV),
            const(1, D_RWKV),
        ],
        out_specs=[
            pl.BlockSpec((T, nb, D_S5), lambda b, i: (i, b, 0)),
            seq(), seq(), seq(), seq(), seq(),
            pl.BlockSpec((nb, 1, 1, D_RWKV), lambda b, i: (b, i, 0, 0)),
            seq(), seq(),
        ],
        out_shape=[
            jax.ShapeDtypeStruct((L, B, D_S5), F32),
            bf, bf, bf, bf, bf,
            jax.ShapeDtypeStruct((B, L // T, 1, D_RWKV), F32),
            f32, f32,
        ],
        scratch_shapes=[pltpu.VMEM((nb, 1, N_RW_COLS), F32)],
        compiler_params=pltpu.CompilerParams(
            dimension_semantics=("parallel", "arbitrary"), vmem_limit_bytes=VMEM_LIMIT),
        name="inproj",
    )(x, g, w_in, mu, w0, wwa, a0, g_up, k_k, k_a, r_k)


def _s5_kernel(u_ref, are_ref, aim_ref, b4_ref, c2_ref, d_ref, wglu_ref, bglu_ref,
               y_ref, s_scr, bu_scr, sb_scr, *, T, NG):
    rows = T * S5_BATCH
    pair_cols = 2 * S5_BLOCK_COLS

    @pl.when(pl.program_id(1) == 0)
    def _():
        s_scr[...] = jnp.zeros_like(s_scr)

    us = [u_ref[:, g * S5_BATCH:(g + 1) * S5_BATCH, :].reshape(rows, D_S5) for g in range(NG)]
    ubs = [u.astype(BF16) for u in us]

    def bu(g, h):
        for q in (2 * h, 2 * h + 1):
            bu_scr[g, :, q * S5_BLOCK_COLS:(q + 1) * S5_BLOCK_COLS] = jnp.dot(
                ubs[g][:, q * LANES:(q + 1) * LANES], b4_ref[q], preferred_element_type=F32)

    def scan(g, h):
        qs = (2 * h, 2 * h + 1)
        a_bc = {}
        st = {}
        for q in qs:
            a0 = q * S5_BLOCK_STATE
            a_bc[q] = (
                jnp.broadcast_to(are_ref[:, a0:a0 + S5_BLOCK_STATE], (S5_BATCH, S5_BLOCK_STATE)),
                jnp.broadcast_to(aim_ref[:, a0:a0 + S5_BLOCK_STATE], (S5_BATCH, S5_BLOCK_STATE)))
            re0 = q * S5_BLOCK_COLS
            st[q] = (s_scr[g, :, re0:re0 + S5_BLOCK_STATE],
                     s_scr[g, :, re0 + S5_BLOCK_STATE:re0 + S5_BLOCK_COLS])
        held = {}
        for t in range(T):
            r0 = t * S5_BATCH
            for q in qs:
                re0 = q * S5_BLOCK_COLS
                im0 = re0 + S5_BLOCK_STATE
                are, aim = a_bc[q]
                sre, sim = st[q]
                nre = are * sre - aim * sim + bu_scr[g, r0:r0 + S5_BATCH, re0:re0 + S5_BLOCK_STATE]
                nim = are * sim + aim * sre + bu_scr[g, r0:r0 + S5_BATCH, im0:im0 + S5_BLOCK_STATE]
                st[q] = (nre, nim)
                if t % 2 == 0:
                    held[q] = (nre, nim)
                else:
                    p0 = r0 - S5_BATCH
                    sb_scr[g, p0:p0 + 2 * S5_BATCH, re0:re0 + S5_BLOCK_STATE] = jnp.concatenate(
                        [held[q][0], nre], axis=0).astype(BF16)
                    sb_scr[g, p0:p0 + 2 * S5_BATCH, im0:im0 + S5_BLOCK_STATE] = jnp.concatenate(
                        [held[q][1], nim], axis=0).astype(BF16)
        for q in qs:
            re0 = q * S5_BLOCK_COLS
            s_scr[g, :, re0:re0 + S5_BLOCK_STATE] = st[q][0]
            s_scr[g, :, re0 + S5_BLOCK_STATE:re0 + S5_BLOCK_COLS] = st[q][1]

    def cs(g, h):
        return jnp.dot(sb_scr[g, :, h * pair_cols:(h + 1) * pair_cols], c2_ref[h],
                       preferred_element_type=F32)

    def gelu_in(g, ys):
        return jax.nn.gelu(jnp.concatenate(ys, axis=-1) + d_ref[...] * us[g])

    def glu_out(g, z):
        gate = jax.nn.sigmoid(_dot(z, wglu_ref[...]) + bglu_ref[...])
        y_ref[:, g * S5_BATCH:(g + 1) * S5_BATCH, :] = (z * gate).reshape(T, S5_BATCH, D_S5)

    n_pairs = S5_BLOCKS // 2
    for h in range(n_pairs):
        bu(0, h)
    ys = {g: [] for g in range(NG)}
    for g in range(NG):
        for h in range(n_pairs):
            scan(g, h)
            if g + 1 < NG:
                bu(g + 1, h)
            ys[g].append(cs(g, h))
            if g > 0 and h == 0:
                glu_out(g - 1, gelu_in(g - 1, ys[g - 1]))
    glu_out(NG - 1, gelu_in(NG - 1, ys[NG - 1]))


def _s5_params(lam_re, lam_im, log_dt, b_re, b_im, c_re, c_im):
    dt = jnp.exp(log_dt)[:, None]
    mag = jnp.exp(lam_re * dt)
    ab_re = mag * jnp.cos(lam_im * dt)
    ab_im = mag * jnp.sin(lam_im * dt)
    den = lam_re * lam_re + lam_im * lam_im
    nr = ab_re - 1.0
    f_re = (nr * lam_re + ab_im * lam_im) / den
    f_im = (ab_im * lam_re - nr * lam_im) / den
    bb_re = f_re[..., None] * b_re - f_im[..., None] * b_im
    bb_im = f_re[..., None] * b_im + f_im[..., None] * b_re
    eye = jnp.eye(S5_BLOCK_GROUPS, dtype=F32)

    def blk(t, q):
        return t[q * S5_BLOCK_GROUPS:(q + 1) * S5_BLOCK_GROUPS]

    def b_block(bb, q):
        return jnp.einsum("gpc,gk->gckp", blk(bb, q), eye).reshape(LANES, S5_BLOCK_STATE)

    def c_block(cc, q):
        return jnp.einsum("gcp,gk->gpkc", blk(cc, q), eye).reshape(S5_BLOCK_STATE, LANES)

    b4 = jnp.stack([jnp.concatenate([b_block(bb_re, q), b_block(bb_im, q)], axis=1)
                    for q in range(S5_BLOCKS)]).astype(BF16)
    c4 = [jnp.concatenate([c_block(c_re, q), -c_block(c_im, q)], axis=0)
          for q in range(S5_BLOCKS)]
    zc = jnp.zeros_like(c4[0])
    c2 = jnp.stack([jnp.concatenate([jnp.concatenate([c4[2 * h], zc], axis=1),
                                     jnp.concatenate([zc, c4[2 * h + 1]], axis=1)], axis=0)
                    for h in range(S5_BLOCKS // 2)]).astype(BF16)
    return ab_re.reshape(1, -1), ab_im.reshape(1, -1), b4, c2


def _s5(u_tm, a_re, a_im, b4, c2, d_skip, w_glu, b_glu):
    L, B, _ = u_tm.shape
    T = min(S5_T, L)
    NG = min(S5_GROUPS_PER_STEP, B // S5_BATCH)
    n_state = S5_BLOCKS * S5_BLOCK_COLS
    const = lambda *shape: pl.BlockSpec(shape, lambda b, t: (0,) * len(shape))
    return pl.pallas_call(
        functools.partial(_s5_kernel, T=T, NG=NG),
        grid=(B // (NG * S5_BATCH), L // T),
        in_specs=[
            pl.BlockSpec((T, NG * S5_BATCH, D_S5), lambda b, t: (t, b, 0)),
            const(1, S5_GROUPS * S5_STATE),
            const(1, S5_GROUPS * S5_STATE),
            const(S5_BLOCKS, LANES, S5_BLOCK_COLS),
            const(S5_BLOCKS // 2, 2 * S5_BLOCK_COLS, 2 * LANES),
            const(1, D_S5),
            const(D_S5, D_S5),
            const(1, D_S5),
        ],
        out_specs=pl.BlockSpec((T, NG * S5_BATCH, D_S5), lambda b, t: (t, b, 0)),
        out_shape=jax.ShapeDtypeStruct((L, B, D_S5), F32),
        scratch_shapes=[
            pltpu.VMEM((NG, S5_BATCH, n_state), F32),
            pltpu.VMEM((NG, T * S5_BATCH, n_state), F32),
            pltpu.VMEM((NG, T * S5_BATCH, n_state), BF16),
        ],
        compiler_params=pltpu.CompilerParams(
            dimension_semantics=("parallel", "arbitrary"), vmem_limit_bytes=VMEM_LIMIT),
        name="s5",
    )(u_tm, a_re, a_im, b4, c2, d_skip, w_glu, b_glu)


def _rwkv_kernel(ah_ref, rh_ref, bh_ref, kh_ref, v_ref, we_ref, bonus_ref, gate_ref, gnw_ref,
                 gnb_ref, o_ref, s_scr, *, C, Bt):
    @pl.when(pl.program_id(1) == 0)
    def _():
        s_scr[...] = jnp.zeros_like(s_scr)

    row = lax.broadcasted_iota(jnp.int32, (C, LANES), 0)
    lane = lax.broadcasted_iota(jnp.int32, (C, LANES), 1)
    col = lane & (RWKV_HEAD - 1)
    head0 = lane < RWKV_HEAD
    strict = col < row
    incl = col <= row
    eye = (col == row).astype(F32)
    blk = row ^ col
    ones_blk = _pair_ones()
    same_head = ones_blk > 0
    sl = [slice(p * LANES, (p + 1) * LANES) for p in range(RW_PAIRS)]
    bis = list(range(Bt))
    ch = [(bi, p) for bi in bis for p in range(RW_PAIRS)]

    def bd(y):
        zero = jnp.zeros_like(y)
        return jnp.concatenate([jnp.where(head0, y, zero), jnp.where(head0, zero, y)], axis=0)

    def bmm(x, y):
        return _dot(x, bd(y))

    nt = (((1,), (1,)), ((), ()))
    tn = (((0,), (0,)), ((), ()))

    def each(fn, *lists):
        return [fn(*args) for args in zip(*lists)]

    ahp = [ah_ref[bi][:, sl[p]] for bi, p in ch]
    rhp = [rh_ref[bi][:, sl[p]] for bi, p in ch]
    bhp = [bh_ref[bi][:, sl[p]] for bi, p in ch]
    khp = [kh_ref[bi][:, sl[p]] for bi, p in ch]
    vp = [v_ref[bi][:, sl[p]] for bi, p in ch]
    wep = [we_ref[bi, 0][:, sl[p]] for bi, p in ch]
    S = [s_scr[bi, p] for bi, p in ch]

    zero = jnp.zeros((C, LANES), BF16)
    lhs = each(lambda x, y: jnp.concatenate([x, y], axis=0), ahp, rhp)
    lhs_m = each(lambda x, y: jnp.concatenate(
        [jnp.where(head0, x, zero), jnp.where(head0, zero, x),
         jnp.where(head0, y, zero), jnp.where(head0, zero, y)], axis=0), ahp, rhp)
    rhs = each(lambda x, y: jnp.concatenate([x, y], axis=0), bhp, khp)
    gram = each(lambda x, y: lax.dot_general(x, y, nt, preferred_element_type=F32), lhs_m, rhs)
    swap = lambda t: pltpu.roll(t, RWKV_HEAD, 1)
    a_ab = [jnp.where(strict, jnp.where(head0, t[:C], swap(t[C:2 * C])), 0.0) for t in gram]
    a_ak = [jnp.where(strict, jnp.where(head0, swap(t[:C]), t[C:2 * C]), 0.0) for t in gram]
    r_b = [jnp.where(incl, jnp.where(head0, t[2 * C:3 * C], swap(t[3 * C:])), 0.0) for t in gram]
    r_k = [jnp.where(incl, jnp.where(head0, swap(t[2 * C:3 * C]), t[3 * C:]), 0.0) for t in gram]
    ss = each(lambda x, st: lax.dot_general(x, st.astype(BF16), nt, preferred_element_type=F32),
              lhs, S)
    akv = each(lambda aa, rk_, vv: bmm(jnp.concatenate([aa, rk_], axis=0), vv), a_ak, r_k, vp)
    def neumann3(q):
        q2 = each(bmm, q, q)
        iq = [eye + t for t in q]
        both = each(lambda x, y: bmm(jnp.concatenate([x, y], axis=0), x), q2, iq)
        return each(lambda x, y: x + y[C:], iq, both), [t[:C] for t in both]

    t8, p4 = neumann3([jnp.where(blk < 8, t, 0.0) for t in a_ab])
    t8 = each(lambda x, y: x + bmm(x, y), t8, p4)
    n1 = each(bmm, t8, [jnp.where(blk < 8, 0.0, t) for t in a_ab])
    y = each(lambda t, s_, kv: bmm(t, s_[:C] + kv[:C]), t8, ss, akv)
    n123, n4 = neumann3(n1)
    z = each(lambda x, w: x + bmm(w, x), y, n4)
    u = each(bmm, n123, z)
    o = each(lambda s_, t, rb, uu: s_[C:] + t[C:] + bmm(rb, uu), ss, akv, r_b, u)
    upd = each(lambda uu, vv, bb, we: lax.dot_general(
        jnp.concatenate([uu.astype(BF16), vv], axis=0), (bb * we).astype(BF16), tn,
        preferred_element_type=F32), u, vp, rhs, wep)
    for (bi, p), st, we, up in zip(ch, S, wep, upd):
        s_scr[bi, p] = st * we + jnp.where(same_head, up, 0.0)
    o = jnp.concatenate(
        [jnp.concatenate(o[bi * RW_PAIRS:(bi + 1) * RW_PAIRS], axis=-1) for bi in bis], axis=0)
    dlt = o - _headsum(o, ones_blk) * (1.0 / RWKV_HEAD)
    var = _headsum(dlt * dlt, ones_blk) * (1.0 / RWKV_HEAD)
    on = dlt * lax.rsqrt(var + GN_EPS) * gnw_ref[...] + gnb_ref[...]
    rows = Bt * C
    out = (on + bonus_ref[...].reshape(rows, D_RWKV)) * gate_ref[...].reshape(rows, D_RWKV)
    o_ref[...] = out.reshape(Bt, C, D_RWKV)


def _rwkv(ah, rh, bh, kh, v, we, bonus, gate, gn_w, gn_b, *, Bt=RW_BT):
    B, L, _ = ah.shape
    C = RW_CHUNK
    Bt = min(Bt, B)
    seq = lambda: pl.BlockSpec((Bt, C, D_RWKV), lambda b, c: (b, c, 0))
    const = lambda *shape: pl.BlockSpec(shape, lambda b, c: (0,) * len(shape))
    return pl.pallas_call(
        functools.partial(_rwkv_kernel, C=C, Bt=Bt),
        grid=(B // Bt, L // C),
        in_specs=[
            seq(), seq(), seq(), seq(), seq(),
            pl.BlockSpec((Bt, 1, 1, D_RWKV), lambda b, c: (b, c, 0, 0)),
            seq(), seq(),
            const(1, D_RWKV),
            const(1, D_RWKV),
        ],
        out_specs=seq(),
        out_shape=jax.ShapeDtypeStruct((B, L, D_RWKV), F32),
        scratch_shapes=[pltpu.VMEM((Bt, RW_PAIRS, LANES, LANES), F32)],
        compiler_params=pltpu.CompilerParams(
            dimension_semantics=("parallel", "arbitrary"), vmem_limit_bytes=VMEM_LIMIT),
        name="rwkv7",
    )(ah, rh, bh, kh, v, we, bonus, gate, gn_w, gn_b)


def _ffn_kernel(ys_ref, yr_ref, x_ref, wo_ref, gpm_ref, gpre_ref, wup_ref, wdn_ref, gpost_ref,
                o_ref):
    n = ys_ref.shape[0] // 2
    halves = [slice(0, n), slice(n, 2 * n)]
    mix = [_dot(ys_ref[s, :], wo_ref[:D_S5, :]) + _dot(yr_ref[0, s, :], wo_ref[D_S5:, :])
           for s in halves]
    h = [x_ref[0, s, :] + _rms(m, gpm_ref[...]) for s, m in zip(halves, mix)]
    up = [_dot(_rms(t, gpre_ref[...]), wup_ref[...]) for t in h]
    ff = [_dot(jnp.square(jnp.maximum(t, 0.0)), wdn_ref[...]) for t in up]
    for s, t, f in zip(halves, h, ff):
        o_ref[0, s, :] = t + _rms(f, gpost_ref[...])


def _ffn(ys_tm, yr, x, w_out, g_post_mix, g_pre_mlp, w_up, w_down, g_post_mlp):
    B, L, _ = x.shape
    tl = min(ROW_TILE, L)
    const = lambda *shape: pl.BlockSpec(shape, lambda b, i: (0,) * len(shape),
                                        pipeline_mode=pl.Buffered(1))
    return pl.pallas_call(
        _ffn_kernel,
        grid=(B, L // tl),
        in_specs=[
            pl.BlockSpec((tl, D_S5), lambda b, i: (i, b)),
            pl.BlockSpec((1, tl, D_RWKV), lambda b, i: (b, i, 0)),
            pl.BlockSpec((1, tl, D_MODEL), lambda b, i: (b, i, 0)),
            const(D_MODEL, D_MODEL),
            const(1, D_MODEL),
            const(1, D_MODEL),
            const(D_MODEL, D_FF),
            const(D_FF, D_MODEL),
            const(1, D_MODEL),
        ],
        out_specs=pl.BlockSpec((1, tl, D_MODEL), lambda b, i: (b, i, 0)),
        out_shape=jax.ShapeDtypeStruct((B, L, D_MODEL), F32),
        compiler_params=pltpu.CompilerParams(
            dimension_semantics=("parallel", "parallel"), vmem_limit_bytes=VMEM_LIMIT),
        name="ffn",
    )(ys_tm.reshape(L, B * D_S5), yr, x, w_out, g_post_mix, g_pre_mlp, w_up, w_down, g_post_mlp)


def _layer(x, g_pre_mix, w_in, s5_lam_re, s5_lam_im, s5_log_dt, s5_b_re, s5_b_im, s5_c_re, s5_c_im,
           s5_d, s5_w_glu, s5_b_glu, rw_mu, rw_w0, rw_w_up, rw_a0, rw_a_up, rw_g_up, rw_k_k, rw_k_a,
           rw_r_k, rw_gn_w, rw_gn_b, w_out, g_post_mix, g_pre_mlp, w_ff_up, w_ff_down, g_post_mlp):
    row = lambda t: t.reshape(1, -1)
    zeros = jnp.zeros((W_RANK, D_RWKV), F32)
    wwa = jnp.concatenate([jnp.concatenate([rw_w_up, zeros], axis=1),
                           jnp.concatenate([zeros, rw_a_up], axis=1)], axis=0).astype(BF16)
    u_tm, *rw_ops = _inproj(x, row(g_pre_mix), w_in.astype(BF16), row(rw_mu), row(rw_w0), wwa,
                            row(rw_a0), rw_g_up.astype(BF16), row(rw_k_k), row(rw_k_a), row(rw_r_k))
    a_re, a_im, b4, c2 = _s5_params(s5_lam_re, s5_lam_im, s5_log_dt, s5_b_re, s5_b_im,
                                    s5_c_re, s5_c_im)
    ys_tm = _s5(u_tm, a_re, a_im, b4, c2, row(s5_d), s5_w_glu.astype(BF16), row(s5_b_glu))
    yr = _rwkv(*rw_ops, row(rw_gn_w), row(rw_gn_b))
    return _ffn(ys_tm, yr, x, w_out.astype(BF16), row(g_post_mix), row(g_pre_mlp),
                w_ff_up.astype(BF16), w_ff_down.astype(BF16), row(g_post_mlp))


def kernel(x, g_pre_mix, w_in, s5_lam_re, s5_lam_im, s5_log_dt, s5_b_re, s5_b_im, s5_c_re, s5_c_im,
           s5_d, s5_w_glu, s5_b_glu, rw_mu, rw_w0, rw_w_up, rw_a0, rw_a_up, rw_g_up, rw_k_k, rw_k_a,
           rw_r_k, rw_gn_w, rw_gn_b, w_out, g_post_mix, g_pre_mlp, w_ff_up, w_ff_down, g_post_mlp):
    params = (g_pre_mix, w_in, s5_lam_re, s5_lam_im, s5_log_dt, s5_b_re, s5_b_im, s5_c_re, s5_c_im,
              s5_d, s5_w_glu, s5_b_glu, rw_mu, rw_w0, rw_w_up, rw_a0, rw_a_up, rw_g_up, rw_k_k,
              rw_k_a, rw_r_k, rw_gn_w, rw_gn_b, w_out, g_post_mix, g_pre_mlp, w_ff_up, w_ff_down,
              g_post_mlp)
    h = x
    for layer in range(g_pre_mix.shape[0]):
        h = _layer(h, *[p[layer] for p in params])
    return h
```
